```python
import jax, jax.numpy as jnp
from jax import lax
import numpy as np

D_MODEL = 2048
BATCH = 4
SEQ = 4096
DEPTH = 2
DEC_BATCH = 1
DEC_SEQ = 16384
PAST_LEN = 128

HEAD_DIM = 128
N_Q_HEADS = D_MODEL // 256
N_KV_HEADS = N_Q_HEADS // 4
Q_GROUP = N_Q_HEADS // N_KV_HEADS
ATTN_WIDTH = N_Q_HEADS * HEAD_DIM
KV_WIDTH = N_KV_HEADS * HEAD_DIM
FOURIER_GROUP_DIM = 128
N_FOURIER_GROUPS = D_MODEL // 256
FOURIER_WIDTH = N_FOURIER_GROUPS * FOURIER_GROUP_DIM
MIX_WIDTH = ATTN_WIDTH + FOURIER_WIDTH
IN_WIDTH = ATTN_WIDTH + 2 * KV_WIDTH + FOURIER_WIDTH
D_FF = 5632
N_EXPERTS = 8
TOP_K = 2
GRID_W = 64
ROPE_THETA = 10000.0
ROPE_PAIRS = HEAD_DIM // 4
Q_BLOCK = 128
EPS = 1e-6
N_DENSE = (DEPTH + 1) // 2
N_MOE = DEPTH // 2

kernel_name = "hybrid_gqa_fnet_encoder"


def rmsnorm(x, g):
    xf = x.astype(jnp.float32)
    y = xf * lax.rsqrt(jnp.mean(xf * xf, axis=-1, keepdims=True) + EPS)
    return (y * g.astype(jnp.float32)).astype(x.dtype)


def grid_rope_angles(seq_len):
    rows = seq_len // GRID_W
    row = jnp.repeat(jnp.arange(rows, dtype=jnp.float32), GRID_W)
    col = jnp.tile(jnp.arange(GRID_W, dtype=jnp.float32), rows)
    inv_freq = ROPE_THETA ** (-jnp.arange(ROPE_PAIRS, dtype=jnp.float32) / ROPE_PAIRS)
    ang = jnp.stack([row[:, None] * inv_freq, col[:, None] * inv_freq], axis=1)
    return jnp.cos(ang), jnp.sin(ang)


def apply_grid_rope(x, cos, sin):
    shp = x.shape
    xr = x.astype(jnp.float32).reshape(shp[:-1] + (2, 2, ROPE_PAIRS))
    x1 = xr[..., 0, :]
    x2 = xr[..., 1, :]
    c = cos[:, None]
    s = sin[:, None]
    out = jnp.stack([x1 * c - x2 * s, x2 * c + x1 * s], axis=-2)
    return out.reshape(shp).astype(x.dtype)


def block_attention(q, k, v):
    b, s = q.shape[0], q.shape[1]
    nb = s // Q_BLOCK
    qb = q.reshape(b, nb, Q_BLOCK, N_KV_HEADS, Q_GROUP, HEAD_DIM).transpose(1, 0, 2, 3, 4, 5)
    scale = HEAD_DIM ** -0.5

    def one_block(q_blk):
        sc = jnp.einsum('bqkgd,bskd->bkgqs', q_blk, k, preferred_element_type=jnp.float32) * scale
        p = jax.nn.softmax(sc, axis=-1).astype(v.dtype)
        return jnp.einsum('bkgqs,bskd->bqkgd', p, v)

    o = lax.map(one_block, qb)
    return o.transpose(1, 0, 2, 3, 4, 5).reshape(b, s, ATTN_WIDTH)


def fourier_mix(u, w_f):
    b, s = u.shape[0], u.shape[1]
    uf = u.reshape(b, s, N_FOURIER_GROUPS, FOURIER_GROUP_DIM).astype(jnp.float32)
    f = jnp.fft.fft2(uf, axes=(1, 3), norm='ortho').real.astype(u.dtype)
    y = jnp.einsum('bsgc,gce->bsge', f, w_f)
    return y.reshape(b, s, FOURIER_WIDTH)


def swiglu(h, wg, wu, wd):
    return (jax.nn.silu(h @ wg) * (h @ wu)) @ wd


def moe_swiglu(h, router, wg, wu, wd):
    logits = jnp.einsum('bsd,de->bse', h, router, preferred_element_type=jnp.float32)
    top_vals, top_idx = lax.top_k(logits, TOP_K)
    gates = jax.nn.softmax(top_vals, axis=-1)
    combine = jnp.sum(jax.nn.one_hot(top_idx, N_EXPERTS, dtype=jnp.float32) * gates[..., None], axis=-2)
    combine = combine.astype(h.dtype)
    out = jnp.zeros_like(h)
    for e in range(N_EXPERTS):
        out = out + combine[..., e:e + 1] * swiglu(h, wg[e], wu[e], wd[e])
    return out


def trunk(x, ln1, w_in, q_norm, k_norm, w_fourier, attn_out_norm, fourier_out_norm, w_out, ln2,
          w_gate_dense, w_up_dense, w_down_dense, router, w_gate_exp, w_up_exp, w_down_exp):
    b, s = x.shape[0], x.shape[1]
    cos, sin = grid_rope_angles(s)
    for l in range(DEPTH):
        h = rmsnorm(x, ln1[l])
        proj = h @ w_in[l]
        q, k, v, u = jnp.split(proj, [ATTN_WIDTH, ATTN_WIDTH + KV_WIDTH, ATTN_WIDTH + 2 * KV_WIDTH], axis=-1)
        q = rmsnorm(q.reshape(b, s, N_Q_HEADS, HEAD_DIM), q_norm[l])
        k = rmsnorm(k.reshape(b, s, N_KV_HEADS, HEAD_DIM), k_norm[l])
        v = v.reshape(b, s, N_KV_HEADS, HEAD_DIM)
        q = apply_grid_rope(q, cos, sin)
        k = apply_grid_rope(k, cos, sin)
        a = block_attention(q, k, v)
        f = fourier_mix(u, w_fourier[l])
        mixed = jnp.concatenate([rmsnorm(a, attn_out_norm[l]), rmsnorm(f, fourier_out_norm[l])], axis=-1)
        x = x + mixed @ w_out[l]
        h = rmsnorm(x, ln2[l])
        i = l // 2
        if l % 2 == 0:
            y = swiglu(h, w_gate_dense[i], w_up_dense[i], w_down_dense[i])
        else:
            y = moe_swiglu(h, router[i], w_gate_exp[i], w_up_exp[i], w_down_exp[i])
        x = x + y
    return x


def setup_inputs(seed: int = 0) -> dict:
    key = jax.random.key(seed)
    ks = jax.random.split(key, 20)
    f32 = jnp.float32

    def nrm(k, shape, fan_in):
        return jax.random.normal(k, shape, f32) * (fan_in ** -0.5)

    def gain(k, shape):
        return jnp.ones(shape, f32) + 0.02 * jax.random.normal(k, shape, f32)

    return {
        "x_prompt": jax.random.normal(ks[0], (BATCH, SEQ, D_MODEL), f32),
        "x_sample": jax.random.normal(ks[1], (DEC_BATCH, DEC_SEQ, D_MODEL), f32),
        "ln1": gain(ks[2], (DEPTH, D_MODEL)),
        "w_in": nrm(ks[3], (DEPTH, D_MODEL, IN_WIDTH), D_MODEL),
        "q_norm": gain(ks[4], (DEPTH, HEAD_DIM)),
        "k_norm": gain(ks[5], (DEPTH, HEAD_DIM)),
        "w_fourier": nrm(ks[6], (DEPTH, N_FOURIER_GROUPS, FOURIER_GROUP_DIM, FOURIER_GROUP_DIM), FOURIER_GROUP_DIM),
        "attn_out_norm": gain(ks[7], (DEPTH, ATTN_WIDTH)),
        "fourier_out_norm": gain(ks[8], (DEPTH, FOURIER_WIDTH)),
        "w_out": nrm(ks[9], (DEPTH, MIX_WIDTH, D_MODEL), MIX_WIDTH),
        "ln2": gain(ks[10], (DEPTH, D_MODEL)),
        "w_gate_dense": nrm(ks[11], (N_DENSE, D_MODEL, D_FF), D_MODEL),
        "w_up_dense": nrm(ks[12], (N_DENSE, D_MODEL, D_FF), D_MODEL),
        "w_down_dense": nrm(ks[13], (N_DENSE, D_FF, D_MODEL), D_FF),
        "router": nrm(ks[14], (N_MOE, D_MODEL, N_EXPERTS), D_MODEL),
        "w_gate_exp": nrm(ks[15], (N_MOE, N_EXPERTS, D_MODEL, D_FF), D_MODEL),
        "w_up_exp": nrm(ks[16], (N_MOE, N_EXPERTS, D_MODEL, D_FF), D_MODEL),
        "w_down_exp": nrm(ks[17], (N_MOE, N_EXPERTS, D_FF, D_MODEL), D_FF),
    }


def reference(x_prompt, x_sample, ln1, w_in, q_norm, k_norm, w_fourier, attn_out_norm, fourier_out_norm,
              w_out, ln2, w_gate_dense, w_up_dense, w_down_dense, router, w_gate_exp, w_up_exp, w_down_exp):
    y_prompt = trunk(x_prompt, ln1, w_in, q_norm, k_norm, w_fourier, attn_out_norm, fourier_out_norm, w_out, ln2,
                     w_gate_dense, w_up_dense, w_down_dense, router, w_gate_exp, w_up_exp, w_down_exp)
    y_sample = trunk(x_sample, ln1, w_in, q_norm, k_norm, w_fourier, attn_out_norm, fourier_out_norm, w_out, ln2,
                     w_gate_dense, w_up_dense, w_down_dense, router, w_gate_exp, w_up_exp, w_down_exp)
    return (y_prompt, y_sample)
```

```python
import functools
import math

import numpy as np
import jax
import jax.numpy as jnp
from jax import lax
from jax.experimental import pallas as pl
from jax.experimental.pallas import tpu as pltpu

F32 = jnp.float32
BF16 = jnp.bfloat16

HEAD_DIM = 128
N_Q_HEADS = 8
N_KV_HEADS = 2
Q_GROUP = N_Q_HEADS // N_KV_HEADS
ATTN_WIDTH = N_Q_HEADS * HEAD_DIM
KV_WIDTH = N_KV_HEADS * HEAD_DIM
GROUP_DIM = 128
N_FOURIER_GROUPS = 8
FOURIER_WIDTH = N_FOURIER_GROUPS * GROUP_DIM
N_EXPERTS = 8
GRID_W = 64
ROPE_THETA = 10000.0
ROPE_PAIRS = HEAD_DIM // 4
EPS = 1e-6
LANES = 128
VMEM_LIMIT = 56 * 1024 * 1024


def _params(*sem):
    return pltpu.CompilerParams(dimension_semantics=sem, vmem_limit_bytes=VMEM_LIMIT)


def _rms(x, g):
    return x * lax.rsqrt(jnp.mean(x * x, axis=-1, keepdims=True) + EPS) * g


def _const_spec(shape):
    zeros = (0,) * len(shape)
    return pl.BlockSpec(shape, lambda *_: zeros, pipeline_mode=pl.Buffered(1))


def _in_proj_kernel(x_ref, ln_ref, w_ref, qg_ref, kg_ref, cos_ref, sin_ref, q_ref, k_ref, v_ref, u_ref):
    h = _rms(x_ref[...], ln_ref[...]).astype(BF16)
    proj = jnp.dot(h, w_ref[...], preferred_element_type=F32)
    cos = cos_ref[...]
    sin = sin_ref[...]
    lane = lax.broadcasted_iota(jnp.int32, cos.shape, 1)
    lower = (lane & (2 * ROPE_PAIRS - 1)) < ROPE_PAIRS

    def norm_rope(t, g):
        y = _rms(t, g)
        partner = jnp.where(lower, pltpu.roll(y, HEAD_DIM - ROPE_PAIRS, 1), pltpu.roll(y, ROPE_PAIRS, 1))
        return y * cos + partner * sin

    qg = qg_ref[...]
    kg = kg_ref[...]
    for hd in range(N_Q_HEADS):
        sl = slice(hd * HEAD_DIM, (hd + 1) * HEAD_DIM)
        q_ref[:, sl] = norm_rope(proj[:, sl], qg).astype(q_ref.dtype)
    for hd in range(N_KV_HEADS):
        sl = slice(hd * HEAD_DIM, (hd + 1) * HEAD_DIM)
        k_ref[:, sl] = norm_rope(proj[:, ATTN_WIDTH + hd * HEAD_DIM:ATTN_WIDTH + (hd + 1) * HEAD_DIM], kg).astype(k_ref.dtype)
    v_ref[...] = proj[:, ATTN_WIDTH + KV_WIDTH:ATTN_WIDTH + 2 * KV_WIDTH].astype(v_ref.dtype)
    u_ref[...] = proj[:, ATTN_WIDTH + 2 * KV_WIDTH:].astype(u_ref.dtype)


def _in_proj(x, ln, w, qg, kg, cos, sin, seq, tm):
    t, d = x.shape
    in_width = w.shape[1]
    blocks_per_seq = seq // tm
    row = lambda i: (i, 0)
    pos = lambda i: (i % blocks_per_seq, 0)
    return pl.pallas_call(
        _in_proj_kernel,
        grid=(t // tm,),
        in_specs=[
            pl.BlockSpec((tm, d), row),
            _const_spec((1, d)),
            _const_spec((d, in_width)),
            _const_spec((1, HEAD_DIM)),
            _const_spec((1, HEAD_DIM)),
            pl.BlockSpec((tm, HEAD_DIM), pos),
            pl.BlockSpec((tm, HEAD_DIM), pos),
        ],
        out_specs=[
            pl.BlockSpec((tm, ATTN_WIDTH), row),
            pl.BlockSpec((tm, KV_WIDTH), row),
            pl.BlockSpec((tm, KV_WIDTH), row),
            pl.BlockSpec((tm, FOURIER_WIDTH), row),
        ],
        out_shape=[
            jax.ShapeDtypeStruct((t, ATTN_WIDTH), BF16),
            jax.ShapeDtypeStruct((t, KV_WIDTH), BF16),
            jax.ShapeDtypeStruct((t, KV_WIDTH), BF16),
            jax.ShapeDtypeStruct((t, FOURIER_WIDTH), BF16),
        ],
        compiler_params=_params("parallel"),
        name="in_proj",
    )(x, ln, w, qg, kg, cos, sin)


def _attn_kernel(q_ref, k_ref, v_ref, o_ref, *, tk):
    tq = q_ref.shape[0]
    seq = k_ref.shape[0]
    q = jnp.concatenate([q_ref[:, g * HEAD_DIM:(g + 1) * HEAD_DIM] for g in range(Q_GROUP)], axis=0)
    rows = q.shape[0]

    def body(j, carry):
        m, l, acc = carry
        start = pl.multiple_of(j * tk, tk)
        kj = k_ref[pl.ds(start, tk), :]
        vj = v_ref[pl.ds(start, tk), :]
        s = lax.dot_general(q, kj, (((1,), (1,)), ((), ())), preferred_element_type=F32)
        m_new = jnp.maximum(m, jnp.max(s, axis=-1, keepdims=True))
        alpha = jnp.exp(m - m_new)
        p = jnp.exp(s - m_new)
        l = alpha * l + jnp.sum(p, axis=-1, keepdims=True)
        acc = alpha * acc + jnp.dot(p.astype(BF16), vj, preferred_element_type=F32)
        return m_new, l, acc

    m0 = jnp.full((rows, 1), -jnp.inf, F32)
    l0 = jnp.zeros((rows, 1), F32)
    acc0 = jnp.zeros((rows, HEAD_DIM), F32)
    _, l, acc = lax.fori_loop(0, seq // tk, body, (m0, l0, acc0))
    out = acc / l
    for g in range(Q_GROUP):
        o_ref[:, g * HEAD_DIM:(g + 1) * HEAD_DIM] = out[g * tq:(g + 1) * tq].astype(o_ref.dtype)


def _attention(q, k, v, batch, seq, tq, tk):
    t = q.shape[0]
    qb = seq // tq
    gw = Q_GROUP * HEAD_DIM
    return pl.pallas_call(
        functools.partial(_attn_kernel, tk=tk),
        grid=(batch, N_KV_HEADS, qb),
        in_specs=[
            pl.BlockSpec((tq, gw), lambda b, h, i: (b * qb + i, h)),
            pl.BlockSpec((seq, HEAD_DIM), lambda b, h, i: (b, h)),
            pl.BlockSpec((seq, HEAD_DIM), lambda b, h, i: (b, h)),
        ],
        out_specs=pl.BlockSpec((tq, gw), lambda b, h, i: (b * qb + i, h)),
        out_shape=jax.ShapeDtypeStruct((t, ATTN_WIDTH), BF16),
        compiler_params=_params("parallel", "parallel", "parallel"),
        name="attention",
    )(q, k, v)


def _fourier1_kernel(u_ref, cmat_ref, dft_ref, twr_ref, twi_ref, are_ref, aim_ref):
    n1 = u_ref.shape[0]
    x = u_ref[...]
    zre, zim = [], []
    for g in range(N_FOURIER_GROUPS):
        r = jnp.dot(x[:, g * GROUP_DIM:(g + 1) * GROUP_DIM], cmat_ref[...], preferred_element_type=F32)
        zre.append(r[:, :GROUP_DIM])
        zim.append(r[:, GROUP_DIM:])
    z = jnp.concatenate([jnp.concatenate(zre, axis=1), jnp.concatenate(zim, axis=1)], axis=0).astype(BF16)
    d = jnp.dot(dft_ref[...], z, preferred_element_type=F32)
    dre, dim = d[:n1], d[n1:]
    n2 = pl.program_id(1)
    pick = lax.broadcasted_iota(jnp.int32, twr_ref.shape, 1) == n2
    tr = jnp.sum(jnp.where(pick, twr_ref[...], 0.0), axis=1, keepdims=True)
    ti = jnp.sum(jnp.where(pick, twi_ref[...], 0.0), axis=1, keepdims=True)
    are_ref[...] = (tr * dre - ti * dim).astype(are_ref.dtype)
    aim_ref[...] = (tr * dim + ti * dre).astype(aim_ref.dtype)


def _fourier2_kernel(are_ref, aim_ref, dft_ref, wf_ref, g_ref, o_ref):
    a = jnp.concatenate([are_ref[...], aim_ref[...]], axis=0)
    f = jnp.dot(dft_ref[...], a, preferred_element_type=F32).astype(BF16)
    ys = [jnp.dot(f[:, g * GROUP_DIM:(g + 1) * GROUP_DIM], wf_ref[g], preferred_element_type=F32)
          for g in range(N_FOURIER_GROUPS)]
    o_ref[...] = _rms(jnp.concatenate(ys, axis=1), g_ref[...]).astype(o_ref.dtype)


def _fourier_tables(seq):
    n1 = 1 << (int(math.log2(seq)) // 2)
    n2 = seq // n1
    c = np.arange(GROUP_DIM)
    ang = 2 * np.pi * np.outer(c, c) / GROUP_DIM
    cmat = np.concatenate([np.cos(ang), -np.sin(ang)], axis=1) / math.sqrt(GROUP_DIM)
    a1 = 2 * np.pi * np.outer(np.arange(n1), np.arange(n1)) / n1
    d1 = np.block([[np.cos(a1), np.sin(a1)], [-np.sin(a1), np.cos(a1)]]) / math.sqrt(n1)
    psi = 2 * np.pi * np.outer(np.arange(n1), np.arange(n2)) / seq
    pad = (-n2) % LANES
    twr = np.pad(np.cos(psi), ((0, 0), (0, pad)))
    twi = np.pad(-np.sin(psi), ((0, 0), (0, pad)))
    a2 = 2 * np.pi * np.outer(np.arange(n2), np.arange(n2)) / n2
    d2 = np.concatenate([np.cos(a2), np.sin(a2)], axis=1) / math.sqrt(n2)
    return (n1, n2, jnp.asarray(cmat, BF16), jnp.asarray(d1, BF16), jnp.asarray(twr, F32),
            jnp.asarray(twi, F32), jnp.asarray(d2, BF16))


def _fourier(u, w_f, gain, batch, seq):
    t, width = u.shape
    n1, n2, cmat, d1, twr, twi, d2 = _fourier_tables(seq)
    u2 = u.reshape(batch * n1, n2 * width)
    are, aim = pl.pallas_call(
        _fourier1_kernel,
        grid=(batch, n2),
        in_specs=[
            pl.BlockSpec((n1, width), lambda b, j: (b, j)),
            _const_spec(cmat.shape),
            _const_spec(d1.shape),
            _const_spec(twr.shape),
            _const_spec(twi.shape),
        ],
        out_specs=[pl.BlockSpec((n1, width), lambda b, j: (b, j))] * 2,
        out_shape=[jax.ShapeDtypeStruct((batch * n1, n2 * width), BF16)] * 2,
        compiler_params=_params("parallel", "parallel"),
        name="fourier_stage1",
    )(u2, cmat, d1, twr, twi)
    are = are.reshape(t, width)
    aim = aim.reshape(t, width)
    out = pl.pallas_call(
        _fourier2_kernel,
        grid=(batch, n1),
        in_specs=[
            pl.BlockSpec((n2, width), lambda b, j: (b * n1 + j, 0)),
            pl.BlockSpec((n2, width), lambda b, j: (b * n1 + j, 0)),
            _const_spec(d2.shape),
            _const_spec(w_f.shape),
            _const_spec(gain.shape),
        ],
        out_specs=pl.BlockSpec((n2, width), lambda b, j: (b, j)),
        out_shape=jax.ShapeDtypeStruct((batch * n2, n1 * width), BF16),
        compiler_params=_params("parallel", "parallel"),
        name="fourier_stage2",
    )(are, aim, d2, w_f, gain)
    return out.reshape(t, width)


def _out_proj_kernel(a_ref, f_ref, x_ref, ag_ref, w_ref, ln2_ref, *rest, routed):
    if routed:
        router_ref, x1_ref, h2_ref, sel_ref, gate_ref = rest
    else:
        x1_ref, h2_ref = rest
    an = _rms(a_ref[...].astype(F32), ag_ref[...]).astype(BF16)
    y = jnp.dot(an, w_ref[:ATTN_WIDTH, :], preferred_element_type=F32)
    y = y + jnp.dot(f_ref[...], w_ref[ATTN_WIDTH:, :], preferred_element_type=F32)
    x1 = x_ref[...] + y
    x1_ref[...] = x1
    h2 = _rms(x1, ln2_ref[...])
    h2_ref[...] = h2.astype(h2_ref.dtype)
    if routed:
        logits = jnp.dot(h2, router_ref[...], preferred_element_type=F32, precision=lax.Precision.HIGHEST)
        lane = lax.broadcasted_iota(jnp.int32, logits.shape, 1)
        lg = jnp.where(lane < N_EXPERTS, logits, -jnp.inf)
        m1 = jnp.max(lg, axis=-1, keepdims=True)
        i1 = jnp.min(jnp.where(lg == m1, lane, LANES), axis=-1, keepdims=True)
        lg2 = jnp.where(lane == i1, -jnp.inf, lg)
        m2 = jnp.max(lg2, axis=-1, keepdims=True)
        i2 = jnp.min(jnp.where(lg2 == m2, lane, LANES), axis=-1, keepdims=True)
        e2 = jnp.exp(m2 - m1)
        g1 = 1.0 / (1.0 + e2)
        g2 = e2 * g1
        sel_ref[...] = jnp.where(lane == 0, i1, jnp.where(lane == 1, i2, 0))
        gate_ref[...] = jnp.where(lane == 0, g1, jnp.where(lane == 1, g2, 0.0))


def _out_proj(a, f, x, ag, w, ln2, router, tm):
    t, d = x.shape
    routed = router is not None
    row = lambda i: (i, 0)
    in_specs = [
        pl.BlockSpec((tm, ATTN_WIDTH), row),
        pl.BlockSpec((tm, FOURIER_WIDTH), row),
        pl.BlockSpec((tm, d), row),
        _const_spec((1, ATTN_WIDTH)),
        _const_spec(w.shape),
        _const_spec((1, d)),
    ]
    out_specs = [pl.BlockSpec((tm, d), row), pl.BlockSpec((tm, d), row)]
    out_shape = [jax.ShapeDtypeStruct((t, d), F32), jax.ShapeDtypeStruct((t, d), F32 if routed else BF16)]
    args = [a, f, x, ag, w, ln2]
    if routed:
        in_specs.append(_const_spec(router.shape))
        out_specs += [pl.BlockSpec((tm, LANES), row)] * 2
        out_shape += [jax.ShapeDtypeStruct((t, LANES), jnp.int32), jax.ShapeDtypeStruct((t, LANES), F32)]
        args.append(router)
    return pl.pallas_call(
        functools.partial(_out_proj_kernel, routed=routed),
        grid=(t // tm,),
        in_specs=in_specs,
        out_specs=out_specs,
        out_shape=out_shape,
        compiler_params=_params("parallel"),
        name="out_proj_routed" if routed else "out_proj",
    )(*args)


def _ffn_kernel(te_ref, na_ref, h_ref, wg_ref, wu_ref, wd_ref, *rest, residual):
    if residual:
        x_ref, o_ref, hb_ref, acc_ref = rest
    else:
        o_ref, hb_ref, acc_ref = rest
    i = pl.program_id(0)
    j = pl.program_id(1)
    last = pl.num_programs(1) - 1
    active = i < na_ref[0]

    @pl.when(active)
    def _():
        @pl.when(j == 0)
        def _():
            hb_ref[...] = h_ref[...].astype(BF16)

        h = hb_ref[...]
        g = jnp.dot(h, wg_ref[0], preferred_element_type=F32)
        u = jnp.dot(h, wu_ref[0], preferred_element_type=F32)
        act = (g * jax.nn.sigmoid(g) * u).astype(BF16)
        part = jnp.dot(act, wd_ref[0], preferred_element_type=F32)

        @pl.when(j == 0)
        def _():
            acc_ref[...] = part

        @pl.when(j > 0)
        def _():
            acc_ref[...] += part

        @pl.when(j == last)
        def _():
            y = acc_ref[...]
            if residual:
                y = y + x_ref[...]
            o_ref[...] = y.astype(o_ref.dtype)

    @pl.when(jnp.logical_and(jnp.logical_not(active), j == last))
    def _():
        o_ref[...] = jnp.zeros_like(o_ref)


def _ffn(h, wg, wu, wd, tile_expert, n_active, x, tm, tf):
    r, d = h.shape
    d_ff = wg.shape[2]
    nj = d_ff // tf
    residual = x is not None

    def jj(i, j, na):
        return jnp.where(i < na[0], j, nj - 1)

    in_specs = [
        pl.BlockSpec((tm, d), lambda i, j, te, na: (i, 0)),
        pl.BlockSpec((1, d, tf), lambda i, j, te, na: (te[i], 0, jj(i, j, na))),
        pl.BlockSpec((1, d, tf), lambda i, j, te, na: (te[i], 0, jj(i, j, na))),
        pl.BlockSpec((1, tf, d), lambda i, j, te, na: (te[i], jj(i, j, na), 0)),
    ]
    args = [h, wg, wu, wd]
    if residual:
        in_specs.append(pl.BlockSpec((tm, d), lambda i, j, te, na: (i, 0)))
        args.append(x)
    return pl.pallas_call(
        functools.partial(_ffn_kernel, residual=residual),
        grid_spec=pltpu.PrefetchScalarGridSpec(
            num_scalar_prefetch=2,
            grid=(r // tm, nj),
            in_specs=in_specs,
            out_specs=pl.BlockSpec((tm, d), lambda i, j, te, na: (i, 0)),
            scratch_shapes=[pltpu.VMEM((tm, d), BF16), pltpu.VMEM((tm, d), F32)],
        ),
        out_shape=jax.ShapeDtypeStruct((r, d), F32),
        compiler_params=_params("parallel", "arbitrary"),
        name="ffn_dense" if residual else "ffn_experts",
    )(tile_expert, n_active, *args)


def _gather_kernel(src_ref, h_ref, o_ref, sem, *, rows):
    base = pl.program_id(0) * rows

    def row_copy(r):
        return pltpu.make_async_copy(h_ref.at[pl.ds(src_ref[base + r], 1)], o_ref.at[pl.ds(base + r, 1)], sem)

    def start(r, c):
        row_copy(r).start()
        return c

    def wait(r, c):
        row_copy(r).wait()
        return c

    lax.fori_loop(0, rows, start, 0)
    lax.fori_loop(0, rows, wait, 0)


def _gather_rows(h, src, rows):
    n = src.shape[0]
    d = h.shape[1]
    return pl.pallas_call(
        functools.partial(_gather_kernel, rows=rows),
        grid_spec=pltpu.PrefetchScalarGridSpec(
            num_scalar_prefetch=1,
            grid=(n // rows,),
            in_specs=[pl.BlockSpec(memory_space=pl.ANY)],
            out_specs=pl.BlockSpec(memory_space=pl.ANY),
            scratch_shapes=[pltpu.SemaphoreType.DMA(())],
        ),
        out_shape=jax.ShapeDtypeStruct((n, d), h.dtype),
        compiler_params=_params("arbitrary"),
        name="gather_rows",
    )(src, h)


def _combine_kernel(pos_ref, y_ref, x_ref, gate_ref, o_ref, buf_ref, sem):
    tc = x_ref.shape[0]
    base = pl.program_id(0) * tc

    def row_copy(r, k):
        return pltpu.make_async_copy(y_ref.at[pl.ds(pos_ref[2 * (base + r) + k], 1)],
                                     buf_ref.at[k, pl.ds(r, 1)], sem)

    def start(r, c):
        row_copy(r, 0).start()
        row_copy(r, 1).start()
        return c

    def wait(r, c):
        row_copy(r, 0).wait()
        row_copy(r, 1).wait()
        return c

    lax.fori_loop(0, tc, start, 0)
    lax.fori_loop(0, tc, wait, 0)
    gate = gate_ref[...]
    o_ref[...] = x_ref[...] + gate[:, 0:1] * buf_ref[0] + gate[:, 1:2] * buf_ref[1]


def _combine(y, pos, x, gate, tc):
    t, d = x.shape
    return pl.pallas_call(
        _combine_kernel,
        grid_spec=pltpu.PrefetchScalarGridSpec(
            num_scalar_prefetch=1,
            grid=(t // tc,),
            in_specs=[
                pl.BlockSpec(memory_space=pl.ANY),
                pl.BlockSpec((tc, d), lambda i, pos: (i, 0)),
                pl.BlockSpec((tc, LANES), lambda i, pos: (i, 0)),
            ],
            out_specs=pl.BlockSpec((tc, d), lambda i, pos: (i, 0)),
            scratch_shapes=[pltpu.VMEM((2, tc, d), F32), pltpu.SemaphoreType.DMA(())],
        ),
        out_shape=jax.ShapeDtypeStruct((t, d), F32),
        compiler_params=_params("arbitrary"),
        name="combine",
    )(pos, y, x, gate)


def _routing_plan(sel, tm):
    t = sel.shape[0]
    experts = sel[:, :2].reshape(-1)
    onehot = (experts[:, None] == jnp.arange(N_EXPERTS, dtype=jnp.int32)[None, :]).astype(jnp.int32)
    csum = jnp.cumsum(onehot, axis=0)
    rank = jnp.sum((csum - onehot) * onehot, axis=1)
    counts = csum[-1]
    padded = ((counts + tm - 1) // tm) * tm
    ends = jnp.cumsum(padded)
    starts = ends - padded
    pos = (starts[experts] + rank).astype(jnp.int32)
    n_tiles = (2 * t) // tm + N_EXPERTS
    src = jnp.zeros((n_tiles * tm,), jnp.int32).at[pos].set(jnp.arange(2 * t, dtype=jnp.int32) // 2)
    tile_start = jnp.arange(n_tiles, dtype=jnp.int32) * tm
    tile_expert = jnp.sum((tile_start[:, None] >= ends[None, :]).astype(jnp.int32), axis=1)
    n_active = (ends[-1] // tm).astype(jnp.int32).reshape(1)
    last_expert = jnp.max(jnp.where(counts > 0, jnp.arange(N_EXPERTS, dtype=jnp.int32), 0))
    tile_expert = jnp.minimum(tile_expert, last_expert).astype(jnp.int32)
    return pos, src, tile_expert, n_active


def _rope_tables(seq):
    t = np.arange(seq)
    inv_freq = ROPE_THETA ** (-np.arange(ROPE_PAIRS, dtype=np.float64) / ROPE_PAIRS)
    row = (t // GRID_W)[:, None] * inv_freq
    col = (t % GRID_W)[:, None] * inv_freq
    cos = np.concatenate([np.cos(row), np.cos(row), np.cos(col), np.cos(col)], axis=1)
    sin = np.concatenate([-np.sin(row), np.sin(row), -np.sin(col), np.sin(col)], axis=1)
    return jnp.asarray(cos, F32), jnp.asarray(sin, F32)


def _tiles(seq):
    tm = min(512, seq)
    return dict(tm=tm, tq=min(256, seq), tk=min(512, seq), tf=512, te=min(512, seq), tc=min(256, seq), gr=min(512, seq))


def _trunk(x3, wts):
    batch, seq, d = x3.shape
    t = batch * seq
    x = x3.reshape(t, d)
    cfg = _tiles(seq)
    cos, sin = _rope_tables(seq)
    depth = wts["ln1"].shape[0]
    for l in range(depth):
        q, k, v, u = _in_proj(x, wts["ln1"][l], wts["w_in"][l], wts["qg"][l], wts["kg"][l], cos, sin, seq, cfg["tm"])
        a = _attention(q, k, v, batch, seq, cfg["tq"], cfg["tk"])
        f = _fourier(u, wts["w_fourier"][l], wts["fourier_out_norm"][l], batch, seq)
        i = l // 2
        if l % 2 == 0:
            x1, h2 = _out_proj(a, f, x, wts["attn_out_norm"][l], wts["w_out"][l], wts["ln2"][l], None, cfg["tm"])
            n_tiles = t // cfg["te"]
            x = _ffn(h2, wts["wg_dense"][i][None], wts["wu_dense"][i][None], wts["wd_dense"][i][None],
                     jnp.zeros((n_tiles,), jnp.int32), jnp.full((1,), n_tiles, jnp.int32), x1, cfg["te"], cfg["tf"])
        else:
            x1, h2, sel, gate = _out_proj(a, f, x, wts["attn_out_norm"][l], wts["w_out"][l], wts["ln2"][l],
                                          wts["router"][i], cfg["tm"])
            pos, src, tile_expert, n_active = _routing_plan(sel, cfg["te"])
            hs = _gather_rows(h2, src, cfg["gr"])
            ys = _ffn(hs, wts["wg_exp"][i], wts["wu_exp"][i], wts["wd_exp"][i], tile_expert, n_active, None,
                      cfg["te"], cfg["tf"])
            x = _combine(ys, pos, x1, gate, cfg["tc"])
    return x.reshape(batch, seq, d)


def _prepare_weights(ln1, w_in, q_norm, k_norm, w_fourier, attn_out_norm, fourier_out_norm, w_out, ln2,
                     w_gate_dense, w_up_dense, w_down_dense, router, w_gate_exp, w_up_exp, w_down_exp):
    depth = ln1.shape[0]
    row = lambda g: g.reshape(depth, 1, -1)
    router_padded = jnp.pad(router, ((0, 0), (0, 0), (0, LANES - router.shape[-1])))
    return dict(
        ln1=row(ln1), w_in=w_in.astype(BF16),
        qg=row(q_norm * (HEAD_DIM ** -0.5)),
        kg=row(k_norm),
        w_fourier=w_fourier.astype(BF16), attn_out_norm=row(attn_out_norm), fourier_out_norm=row(fourier_out_norm),
        w_out=w_out.astype(BF16), ln2=row(ln2),
        wg_dense=w_gate_dense.astype(BF16), wu_dense=w_up_dense.astype(BF16), wd_dense=w_down_dense.astype(BF16),
        router=router_padded,
        wg_exp=w_gate_exp.astype(BF16), wu_exp=w_up_exp.astype(BF16), wd_exp=w_down_exp.astype(BF16),
    )


def kernel(x_prompt, x_sample, ln1, w_in, q_norm, k_norm, w_fourier, attn_out_norm, fourier_out_norm, w_out, ln2,
           w_gate_dense, w_up_dense, w_down_dense, router, w_gate_exp, w_up_exp, w_down_exp):
    wts = _prepare_weights(ln1, w_in, q_norm, k_norm, w_fourier, attn_out_norm, fourier_out_norm, w_out, ln2,
                           w_gate_dense, w_up_dense, w_down_dense, router, w_gate_exp, w_up_exp, w_down_exp)
    return _trunk(x_prompt, wts), _trunk(x_sample, wts)
```

```python
import functools
import math

import numpy as np
import jax
import jax.numpy as jnp
from jax import lax
from jax.experimental import pallas as pl
from jax.experimental.pallas import tpu as pltpu

F32 = jnp.float32
BF16 = jnp.bfloat16

HEAD_DIM = 128
N_Q_HEADS = 8
N_KV_HEADS = 2
Q_GROUP = N_Q_HEADS // N_KV_HEADS
ATTN_WIDTH = N_Q_HEADS * HEAD_DIM
KV_WIDTH = N_KV_HEADS * HEAD_DIM
GROUP_DIM = 128
N_FOURIER_GROUPS = 8
FOURIER_WIDTH = N_FOURIER_GROUPS * GROUP_DIM
N_EXPERTS = 8
GRID_W = 64
ROPE_THETA = 10000.0
ROPE_PAIRS = HEAD_DIM // 4
EPS = 1e-6
LOG2_E = 1.4426950408889634
MAX_UNSHIFTED_SCORE = 48.0
FAST_UNROLL = 8
LANES = 128
VMEM_LIMIT = 56 * 1024 * 1024


def _params(*sem):
    return pltpu.CompilerParams(dimension_semantics=sem, vmem_limit_bytes=VMEM_LIMIT)


def _rms(x, g):
    return x * lax.rsqrt(jnp.mean(x * x, axis=-1, keepdims=True) + EPS) * g


def _const_spec(shape):
    zeros = (0,) * len(shape)
    return pl.BlockSpec(shape, lambda *_: zeros, pipeline_mode=pl.Buffered(1))


def _in_proj_kernel(x_ref, ln_ref, w_ref, qg_ref, kg_ref, vs_ref, cos_ref, sin_ref, q_ref, k_ref, v_ref, u_ref):
    h = _rms(x_ref[...], ln_ref[...]).astype(BF16)
    proj = jnp.dot(h, w_ref[...], preferred_element_type=F32)
    cos = cos_ref[...]
    sin = sin_ref[...]
    lane = lax.broadcasted_iota(jnp.int32, cos.shape, 1)
    lower = (lane & (2 * ROPE_PAIRS - 1)) < ROPE_PAIRS

    def norm_rope(t, g):
        y = _rms(t, g)
        partner = jnp.where(lower, pltpu.roll(y, HEAD_DIM - ROPE_PAIRS, 1), pltpu.roll(y, ROPE_PAIRS, 1))
        return y * cos + partner * sin

    qg = qg_ref[...]
    kg = kg_ref[...]
    for hd in range(N_Q_HEADS):
        sl = slice(hd * HEAD_DIM, (hd + 1) * HEAD_DIM)
        q_ref[:, sl] = norm_rope(proj[:, sl], qg).astype(q_ref.dtype)
    for hd in range(N_KV_HEADS):
        sl = slice(hd * HEAD_DIM, (hd + 1) * HEAD_DIM)
        k_ref[:, sl] = norm_rope(proj[:, ATTN_WIDTH + hd * HEAD_DIM:ATTN_WIDTH + (hd + 1) * HEAD_DIM], kg).astype(k_ref.dtype)
    vs = vs_ref[...]
    ones = jnp.broadcast_to(vs, (proj.shape[0], HEAD_DIM)).astype(v_ref.dtype)
    for hd in range(N_KV_HEADS):
        v0 = ATTN_WIDTH + KV_WIDTH + hd * HEAD_DIM
        v_ref[:, 2 * hd * HEAD_DIM:(2 * hd + 1) * HEAD_DIM] = (proj[:, v0:v0 + HEAD_DIM] * vs).astype(v_ref.dtype)
        v_ref[:, (2 * hd + 1) * HEAD_DIM:(2 * hd + 2) * HEAD_DIM] = ones
    u_ref[...] = proj[:, ATTN_WIDTH + 2 * KV_WIDTH:].astype(u_ref.dtype)


def _in_proj(x, ln, w, qg, kg, vs, cos, sin, seq, tm):
    t, d = x.shape
    in_width = w.shape[1]
    blocks_per_seq = seq // tm
    row = lambda i: (i, 0)
    pos = lambda i: (i % blocks_per_seq, 0)
    return pl.pallas_call(
        _in_proj_kernel,
        grid=(t // tm,),
        in_specs=[
            pl.BlockSpec((tm, d), row),
            _const_spec((1, d)),
            _const_spec((d, in_width)),
            _const_spec((1, HEAD_DIM)),
            _const_spec((1, HEAD_DIM)),
            _const_spec((1, HEAD_DIM)),
            pl.BlockSpec((tm, HEAD_DIM), pos),
            pl.BlockSpec((tm, HEAD_DIM), pos),
        ],
        out_specs=[
            pl.BlockSpec((tm, ATTN_WIDTH), row),
            pl.BlockSpec((tm, KV_WIDTH), row),
            pl.BlockSpec((tm, 2 * KV_WIDTH), row),
            pl.BlockSpec((tm, FOURIER_WIDTH), row),
        ],
        out_shape=[
            jax.ShapeDtypeStruct((t, ATTN_WIDTH), BF16),
            jax.ShapeDtypeStruct((t, KV_WIDTH), BF16),
            jax.ShapeDtypeStruct((t, 2 * KV_WIDTH), BF16),
            jax.ShapeDtypeStruct((t, FOURIER_WIDTH), BF16),
        ],
        compiler_params=_params("parallel"),
        name="in_proj",
    )(x, ln, w, qg, kg, vs, cos, sin)


def _attn_kernel(bounded_ref, q_ref, k_ref, v_ref, o_ref, *, tk):
    tq = q_ref.shape[0]
    seq = k_ref.shape[0]
    q = jnp.concatenate([q_ref[:, g * HEAD_DIM:(g + 1) * HEAD_DIM] for g in range(Q_GROUP)], axis=0)
    rows = q.shape[0]

    def scores(j):
        start = pl.multiple_of(j * tk, tk)
        kj = k_ref[pl.ds(start, tk), :]
        vj = v_ref[pl.ds(start, tk), :]
        return lax.dot_general(q, kj, (((1,), (1,)), ((), ())), preferred_element_type=F32), vj

    def write(out):
        for g in range(Q_GROUP):
            o_ref[:, g * HEAD_DIM:(g + 1) * HEAD_DIM] = out[g * tq:(g + 1) * tq].astype(o_ref.dtype)

    @pl.when(bounded_ref[0] == 1)
    def _():
        def body(j, acc):
            s, vj = scores(j)
            return acc + jnp.dot(jnp.exp2(s).astype(BF16), vj, preferred_element_type=F32)

        acc = lax.fori_loop(0, seq // tk, body, jnp.zeros((rows, 2 * HEAD_DIM), F32), unroll=FAST_UNROLL)
        write(acc[:, :HEAD_DIM] / acc[:, HEAD_DIM:])

    @pl.when(bounded_ref[0] == 0)
    def _():
        def body(j, carry):
            m, acc = carry
            s, vj = scores(j)
            m_new = jnp.maximum(m, jnp.max(s, axis=-1, keepdims=True))
            p = jnp.exp2(s - m_new).astype(BF16)
            return m_new, jnp.exp2(m - m_new) * acc + jnp.dot(p, vj, preferred_element_type=F32)

        m0 = jnp.full((rows, 1), -jnp.inf, F32)
        _, acc = lax.fori_loop(0, seq // tk, body, (m0, jnp.zeros((rows, 2 * HEAD_DIM), F32)))
        write(acc[:, :HEAD_DIM] / acc[:, HEAD_DIM:])


def _attention(bounded, q, k, v1, batch, seq, tq, tk):
    t = q.shape[0]
    qb = seq // tq
    gw = Q_GROUP * HEAD_DIM
    return pl.pallas_call(
        functools.partial(_attn_kernel, tk=tk),
        grid_spec=pltpu.PrefetchScalarGridSpec(
            num_scalar_prefetch=1,
            grid=(batch, N_KV_HEADS, qb),
            in_specs=[
                pl.BlockSpec((tq, gw), lambda b, h, i, f: (b * qb + i, h)),
                pl.BlockSpec((seq, HEAD_DIM), lambda b, h, i, f: (b, h)),
                pl.BlockSpec((seq, 2 * HEAD_DIM), lambda b, h, i, f: (b, h)),
            ],
            out_specs=pl.BlockSpec((tq, gw), lambda b, h, i, f: (b * qb + i, h)),
        ),
        out_shape=jax.ShapeDtypeStruct((t, ATTN_WIDTH), BF16),
        compiler_params=_params("parallel", "parallel", "parallel"),
        name="attention",
    )(bounded, q, k, v1)


def _fourier1_kernel(u_ref, cmat_ref, dft_ref, twr_ref, twi_ref, are_ref, aim_ref):
    n1 = u_ref.shape[0]
    x = u_ref[...]
    zre, zim = [], []
    for g in range(N_FOURIER_GROUPS):
        r = jnp.dot(x[:, g * GROUP_DIM:(g + 1) * GROUP_DIM], cmat_ref[...], preferred_element_type=F32)
        zre.append(r[:, :GROUP_DIM])
        zim.append(r[:, GROUP_DIM:])
    z = jnp.concatenate([jnp.concatenate(zre, axis=1), jnp.concatenate(zim, axis=1)], axis=0).astype(BF16)
    d = jnp.dot(dft_ref[...], z, preferred_element_type=F32)
    dre, dim = d[:n1], d[n1:]
    n2 = pl.program_id(1)
    pick = lax.broadcasted_iota(jnp.int32, twr_ref.shape, 1) == n2
    tr = jnp.sum(jnp.where(pick, twr_ref[...], 0.0), axis=1, keepdims=True)
    ti = jnp.sum(jnp.where(pick, twi_ref[...], 0.0), axis=1, keepdims=True)
    are_ref[...] = (tr * dre - ti * dim).astype(are_ref.dtype)
    aim_ref[...] = (tr * dim + ti * dre).astype(aim_ref.dtype)


def _fourier2_kernel(are_ref, aim_ref, dft_ref, wf_ref, g_ref, o_ref):
    a = jnp.concatenate([are_ref[...], aim_ref[...]], axis=0)
    f = jnp.dot(dft_ref[...], a, preferred_element_type=F32).astype(BF16)
    ys = [jnp.dot(f[:, g * GROUP_DIM:(g + 1) * GROUP_DIM], wf_ref[g], preferred_element_type=F32)
          for g in range(N_FOURIER_GROUPS)]
    o_ref[...] = _rms(jnp.concatenate(ys, axis=1), g_ref[...]).astype(o_ref.dtype)


def _fourier_tables(seq):
    n1 = 1 << (int(math.log2(seq)) // 2)
    n2 = seq // n1
    c = np.arange(GROUP_DIM)
    ang = 2 * np.pi * np.outer(c, c) / GROUP_DIM
    cmat = np.concatenate([np.cos(ang), -np.sin(ang)], axis=1) / math.sqrt(GROUP_DIM)
    a1 = 2 * np.pi * np.outer(np.arange(n1), np.arange(n1)) / n1
    d1 = np.block([[np.cos(a1), np.sin(a1)], [-np.sin(a1), np.cos(a1)]]) / math.sqrt(n1)
    psi = 2 * np.pi * np.outer(np.arange(n1), np.arange(n2)) / seq
    pad = (-n2) % LANES
    twr = np.pad(np.cos(psi), ((0, 0), (0, pad)))
    twi = np.pad(-np.sin(psi), ((0, 0), (0, pad)))
    a2 = 2 * np.pi * np.outer(np.arange(n2), np.arange(n2)) / n2
    d2 = np.concatenate([np.cos(a2), np.sin(a2)], axis=1) / math.sqrt(n2)
    return (n1, n2, jnp.asarray(cmat, BF16), jnp.asarray(d1, BF16), jnp.asarray(twr, F32),
            jnp.asarray(twi, F32), jnp.asarray(d2, BF16))


def _fourier(u, w_f, gain, batch, seq):
    t, width = u.shape
    n1, n2, cmat, d1, twr, twi, d2 = _fourier_tables(seq)
    u2 = u.reshape(batch * n1, n2 * width)
    are, aim = pl.pallas_call(
        _fourier1_kernel,
        grid=(batch, n2),
        in_specs=[
            pl.BlockSpec((n1, width), lambda b, j: (b, j)),
            _const_spec(cmat.shape),
            _const_spec(d1.shape),
            _const_spec(twr.shape),
            _const_spec(twi.shape),
        ],
        out_specs=[pl.BlockSpec((n1, width), lambda b, j: (b, j))] * 2,
        out_shape=[jax.ShapeDtypeStruct((batch * n1, n2 * width), BF16)] * 2,
        compiler_params=_params("parallel", "parallel"),
        name="fourier_stage1",
    )(u2, cmat, d1, twr, twi)
    are = are.reshape(t, width)
    aim = aim.reshape(t, width)
    out = pl.pallas_call(
        _fourier2_kernel,
        grid=(batch, n1),
        in_specs=[
            pl.BlockSpec((n2, width), lambda b, j: (b * n1 + j, 0)),
            pl.BlockSpec((n2, width), lambda b, j: (b * n1 + j, 0)),
            _const_spec(d2.shape),
            _const_spec(w_f.shape),
            _const_spec(gain.shape),
        ],
        out_specs=pl.BlockSpec((n2, width), lambda b, j: (b, j)),
        out_shape=jax.ShapeDtypeStruct((batch * n2, n1 * width), BF16),
        compiler_params=_params("parallel", "parallel"),
        name="fourier_stage2",
    )(are, aim, d2, w_f, gain)
    return out.reshape(t, width)


def _out_proj_kernel(a_ref, f_ref, x_ref, ag_ref, w_ref, ln2_ref, *rest, routed):
    if routed:
        router_ref, x1_ref, h2_ref, sel_ref, gate_ref = rest
    else:
        x1_ref, h2_ref = rest
    an = _rms(a_ref[...].astype(F32), ag_ref[...]).astype(BF16)
    y = jnp.dot(an, w_ref[:ATTN_WIDTH, :], preferred_element_type=F32)
    y = y + jnp.dot(f_ref[...], w_ref[ATTN_WIDTH:, :], preferred_element_type=F32)
    x1 = x_ref[...] + y
    x1_ref[...] = x1
    h2 = _rms(x1, ln2_ref[...])
    h2_ref[...] = h2.astype(h2_ref.dtype)
    if routed:
        h_hi = h2.astype(BF16)
        h_lo = (h2 - h_hi.astype(F32)).astype(BF16)
        logits = (jnp.dot(h_hi, router_ref[0], preferred_element_type=F32)
                  + jnp.dot(h_lo, router_ref[0], preferred_element_type=F32)
                  + jnp.dot(h_hi, router_ref[1], preferred_element_type=F32))
        lane = lax.broadcasted_iota(jnp.int32, logits.shape, 1)
        lg = jnp.where(lane < N_EXPERTS, logits, -jnp.inf)
        m1 = jnp.max(lg, axis=-1, keepdims=True)
        i1 = jnp.min(jnp.where(lg == m1, lane, LANES), axis=-1, keepdims=True)
        lg2 = jnp.where(lane == i1, -jnp.inf, lg)
        m2 = jnp.max(lg2, axis=-1, keepdims=True)
        i2 = jnp.min(jnp.where(lg2 == m2, lane, LANES), axis=-1, keepdims=True)
        e2 = jnp.exp(m2 - m1)
        g1 = 1.0 / (1.0 + e2)
        g2 = e2 * g1
        sel_ref[...] = jnp.where(lane == 0, i1, jnp.where(lane == 1, i2, 0))
        gate_ref[...] = jnp.where(lane == 0, g1, jnp.where(lane == 1, g2, 0.0))


def _out_proj(a, f, x, ag, w, ln2, router, tm):
    t, d = x.shape
    routed = router is not None
    row = lambda i: (i, 0)
    in_specs = [
        pl.BlockSpec((tm, ATTN_WIDTH), row),
        pl.BlockSpec((tm, FOURIER_WIDTH), row),
        pl.BlockSpec((tm, d), row),
        _const_spec((1, ATTN_WIDTH)),
        _const_spec(w.shape),
        _const_spec((1, d)),
    ]
    out_specs = [pl.BlockSpec((tm, d), row), pl.BlockSpec((tm, d), row)]
    out_shape = [jax.ShapeDtypeStruct((t, d), F32), jax.ShapeDtypeStruct((t, d), F32 if routed else BF16)]
    args = [a, f, x, ag, w, ln2]
    if routed:
        in_specs.append(_const_spec(router.shape))
        out_specs += [pl.BlockSpec((tm, LANES), row)] * 2
        out_shape += [jax.ShapeDtypeStruct((t, LANES), jnp.int32), jax.ShapeDtypeStruct((t, LANES), F32)]
        args.append(router)
    return pl.pallas_call(
        functools.partial(_out_proj_kernel, routed=routed),
        grid=(t // tm,),
        in_specs=in_specs,
        out_specs=out_specs,
        out_shape=out_shape,
        compiler_params=_params("parallel"),
        name="out_proj_routed" if routed else "out_proj",
    )(*args)


def _swiglu_accumulate(h_ref, wg_ref, wu_ref, wd_ref, acc_ref):
    h = h_ref[...]
    g = jnp.dot(h, wg_ref[...], preferred_element_type=F32)
    u = jnp.dot(h, wu_ref[...], preferred_element_type=F32)
    act = (g * jax.nn.sigmoid(g) * u).astype(BF16)
    acc_ref[...] += jnp.dot(act, wd_ref[...], preferred_element_type=F32)


def _ffn_dense_kernel(h_ref, wg_ref, wu_ref, wd_ref, x_ref, o_ref, acc_ref):
    j = pl.program_id(1)

    @pl.when(j == 0)
    def _():
        acc_ref[...] = jnp.zeros_like(acc_ref)

    _swiglu_accumulate(h_ref, wg_ref, wu_ref, wd_ref, acc_ref)

    @pl.when(j == pl.num_programs(1) - 1)
    def _():
        o_ref[...] = x_ref[...] + acc_ref[...]


def _ffn_dense(h, wg, wu, wd, x, tm, tf):
    r, d = h.shape
    d_ff = wg.shape[1]
    return pl.pallas_call(
        _ffn_dense_kernel,
        grid=(r // tm, d_ff // tf),
        in_specs=[
            pl.BlockSpec((tm, d), lambda i, j: (i, 0)),
            pl.BlockSpec((d, tf), lambda i, j: (0, j)),
            pl.BlockSpec((d, tf), lambda i, j: (0, j)),
            pl.BlockSpec((tf, d), lambda i, j: (j, 0)),
            pl.BlockSpec((tm, d), lambda i, j: (i, 0)),
        ],
        out_specs=pl.BlockSpec((tm, d), lambda i, j: (i, 0)),
        out_shape=jax.ShapeDtypeStruct((r, d), F32),
        scratch_shapes=[pltpu.VMEM((tm, d), F32)],
        compiler_params=_params("parallel", "arbitrary"),
        name="ffn_dense",
    )(h, wg, wu, wd, x)


def _ffn_experts_kernel(te_ref, na_ref, src_ref, h_ref, wg_ref, wu_ref, wd_ref, o_ref,
                        rows_ref, hb_ref, acc_ref, sem, *, nj):
    i = pl.program_id(0)
    j = pl.program_id(1)
    tm = hb_ref.shape[0]
    n_active = na_ref[0]
    slot = lax.rem(i, 2)
    per_step = -(-tm // nj)
    every_step = tm - per_step * (nj - 1)

    def row_copy(tile, r, s):
        return pltpu.make_async_copy(h_ref.at[pl.ds(src_ref[tile * tm + r], 1)], rows_ref.at[s, pl.ds(r, 1)],
                                     sem.at[s])

    @pl.when(j == 0)
    def _():
        @pl.when(i == 0)
        def _():
            def start(r, c):
                row_copy(0, r, 0).start()
                return c
            lax.fori_loop(0, tm, start, 0)

        @pl.when(i <= n_active)
        def _():
            def wait(r, c):
                row_copy(i, r, slot).wait()
                return c
            lax.fori_loop(0, tm, wait, 0)

    @pl.when(i < n_active)
    def _():
        @pl.when(j == 0)
        def _():
            hb_ref[...] = rows_ref[slot].astype(BF16)
            acc_ref[...] = jnp.zeros_like(acc_ref)

        base = j * per_step

        @pl.when(j < nj - 1)
        def _():
            for r in range(every_step, per_step):
                row_copy(i + 1, base + r, 1 - slot).start()

        for r in range(every_step):
            row_copy(i + 1, base + r, 1 - slot).start()
        _swiglu_accumulate(hb_ref, wg_ref.at[0], wu_ref.at[0], wd_ref.at[0], acc_ref)

        @pl.when(j == nj - 1)
        def _():
            o_ref[...] = acc_ref[...]

    @pl.when(jnp.logical_and(i >= n_active, j == nj - 1))
    def _():
        o_ref[...] = jnp.zeros_like(o_ref)


def _ffn_experts(h, src, wg, wu, wd, tile_expert, n_active, tm, tf):
    d = h.shape[1]
    d_ff = wg.shape[2]
    nj = d_ff // tf
    n_tiles = src.shape[0] // tm

    def jj(i, j, na):
        return jnp.where(i < na[0], j, nj - 1)

    return pl.pallas_call(
        functools.partial(_ffn_experts_kernel, nj=nj),
        grid_spec=pltpu.PrefetchScalarGridSpec(
            num_scalar_prefetch=3,
            grid=(n_tiles, nj),
            in_specs=[
                pl.BlockSpec(memory_space=pl.ANY),
                pl.BlockSpec((1, d, tf), lambda i, j, te, na, src: (te[i], 0, jj(i, j, na))),
                pl.BlockSpec((1, d, tf), lambda i, j, te, na, src: (te[i], 0, jj(i, j, na))),
                pl.BlockSpec((1, tf, d), lambda i, j, te, na, src: (te[i], jj(i, j, na), 0)),
            ],
            out_specs=pl.BlockSpec((tm, d), lambda i, j, te, na, src: (i, 0)),
            scratch_shapes=[pltpu.VMEM((2, tm, d), F32), pltpu.VMEM((tm, d), BF16), pltpu.VMEM((tm, d), F32),
                            pltpu.SemaphoreType.DMA((2,))],
        ),
        out_shape=jax.ShapeDtypeStruct((n_tiles * tm, d), F32),
        compiler_params=_params("arbitrary", "arbitrary"),
        name="ffn_experts",
    )(tile_expert, n_active, src, h, wg, wu, wd)


def _combine_kernel(pos_ref, y_ref, x_ref, gate_ref, o_ref, buf_ref, sem):
    tc = x_ref.shape[0]
    base = pl.program_id(0) * tc

    def row_copy(r, k):
        return pltpu.make_async_copy(y_ref.at[pl.ds(pos_ref[2 * (base + r) + k], 1)],
                                     buf_ref.at[k, pl.ds(r, 1)], sem)

    def start(r, c):
        row_copy(r, 0).start()
        row_copy(r, 1).start()
        return c

    def wait(r, c):
        row_copy(r, 0).wait()
        row_copy(r, 1).wait()
        return c

    lax.fori_loop(0, tc, start, 0)
    lax.fori_loop(0, tc, wait, 0)
    gate = gate_ref[...]
    o_ref[...] = x_ref[...] + gate[:, 0:1] * buf_ref[0] + gate[:, 1:2] * buf_ref[1]


def _combine(y, pos, x, gate, tc):
    t, d = x.shape
    return pl.pallas_call(
        _combine_kernel,
        grid_spec=pltpu.PrefetchScalarGridSpec(
            num_scalar_prefetch=1,
            grid=(t // tc,),
            in_specs=[
                pl.BlockSpec(memory_space=pl.ANY),
                pl.BlockSpec((tc, d), lambda i, pos: (i, 0)),
                pl.BlockSpec((tc, LANES), lambda i, pos: (i, 0)),
            ],
            out_specs=pl.BlockSpec((tc, d), lambda i, pos: (i, 0)),
            scratch_shapes=[pltpu.VMEM((2, tc, d), F32), pltpu.SemaphoreType.DMA(())],
        ),
        out_shape=jax.ShapeDtypeStruct((t, d), F32),
        compiler_params=_params("arbitrary"),
        name="combine",
    )(pos, y, x, gate)


def _routing_plan(sel, tm):
    t = sel.shape[0]
    experts = sel[:, :2].reshape(-1)
    onehot = (experts[:, None] == jnp.arange(N_EXPERTS, dtype=jnp.int32)[None, :]).astype(jnp.int32)
    csum = jnp.cumsum(onehot, axis=0)
    rank = jnp.sum((csum - onehot) * onehot, axis=1)
    counts = csum[-1]
    padded = ((counts + tm - 1) // tm) * tm
    ends = jnp.cumsum(padded)
    starts = ends - padded
    pos = (starts[experts] + rank).astype(jnp.int32)
    n_tiles = (2 * t) // tm + N_EXPERTS + 1
    src =jnp.zeros((n_tiles * tm,), jnp.int32).at[pos].set(jnp.arange(2 * t, dtype=jnp.int32) // 2)
    tile_start = jnp.arange(n_tiles, dtype=jnp.int32) * tm
    tile_expert = jnp.sum((tile_start[:, None] >= ends[None, :]).astype(jnp.int32), axis=1)
    n_active = (ends[-1] // tm).astype(jnp.int32).reshape(1)
    last_expert = jnp.max(jnp.where(counts > 0, jnp.arange(N_EXPERTS, dtype=jnp.int32), 0))
    tile_expert = jnp.minimum(tile_expert, last_expert).astype(jnp.int32)
    return pos, src, tile_expert, n_active


def _rope_tables(seq):
    t = np.arange(seq)
    inv_freq = ROPE_THETA ** (-np.arange(ROPE_PAIRS, dtype=np.float64) / ROPE_PAIRS)
    row = (t // GRID_W)[:, None] * inv_freq
    col = (t % GRID_W)[:, None] * inv_freq
    cos = np.concatenate([np.cos(row), np.cos(row), np.cos(col), np.cos(col)], axis=1)
    sin = np.concatenate([-np.sin(row), np.sin(row), -np.sin(col), np.sin(col)], axis=1)
    return jnp.asarray(cos, F32), jnp.asarray(sin, F32)


def _tiles(seq):
    tm = min(512, seq)
    return dict(tm=tm, tq=min(256, seq), tk=min(512, seq), tf=512, te=min(512, seq), tc=min(256, seq), gr=min(512, seq))


def _trunk(x3, wts):
    batch, seq, d = x3.shape
    t = batch * seq
    x = x3.reshape(t, d)
    cfg = _tiles(seq)
    cos, sin = _rope_tables(seq)
    depth = wts["ln1"].shape[0]
    for l in range(depth):
        q, k, v1, u = _in_proj(x, wts["ln1"][l], wts["w_in"][l], wts["qg"][l], wts["kg"][l], wts["v_scale"][l],
                              cos, sin, seq, cfg["tm"])
        a = _attention(wts["bounded"][l], q, k, v1, batch, seq, cfg["tq"], cfg["tk"])
        f = _fourier(u, wts["w_fourier"][l], wts["fourier_out_norm"][l], batch, seq)
        i = l // 2
        if l % 2 == 0:
            x1, h2 = _out_proj(a, f, x, wts["attn_out_norm"][l], wts["w_out"][l], wts["ln2"][l], None, cfg["tm"])
            x = _ffn_dense(h2, wts["wg_dense"][i], wts["wu_dense"][i], wts["wd_dense"][i], x1, cfg["te"], cfg["tf"])
        else:
            x1, h2, sel, gate = _out_proj(a, f, x, wts["attn_out_norm"][l], wts["w_out"][l], wts["ln2"][l],
                                          wts["router"][i], cfg["tm"])
            pos, src, tile_expert, n_active = _routing_plan(sel, cfg["te"])
            ys = _ffn_experts(h2, src, wts["wg_exp"][i], wts["wu_exp"][i], wts["wd_exp"][i], tile_expert, n_active,
                              cfg["te"], cfg["tf"])
            x = _combine(ys, pos, x1, gate, cfg["tc"])
    return x.reshape(batch, seq, d)


def _prepare_weights(ln1, w_in, q_norm, k_norm, w_fourier, attn_out_norm, fourier_out_norm, w_out, ln2,
                     w_gate_dense, w_up_dense, w_down_dense, router, w_gate_exp, w_up_exp, w_down_exp):
    depth = ln1.shape[0]
    row = lambda g: g.reshape(depth, 1, -1)
    router_padded = jnp.pad(router, ((0, 0), (0, 0), (0, LANES - router.shape[-1])))
    router_hi = router_padded.astype(BF16)
    router_lo = (router_padded - router_hi.astype(F32)).astype(BF16)
    score_bound = (LOG2_E * math.sqrt(HEAD_DIM) * (1 + 2.0 ** -6)
                   * jnp.max(jnp.abs(q_norm), axis=1) * jnp.max(jnp.abs(k_norm), axis=1))
    bounded = score_bound <= MAX_UNSHIFTED_SCORE
    v_scale = jnp.where(bounded, jnp.exp2(-jnp.ceil(score_bound)), 1.0)
    return dict(
        bounded=bounded.astype(jnp.int32).reshape(depth, 1),
        v_scale=jnp.broadcast_to(v_scale.reshape(depth, 1, 1), (depth, 1, HEAD_DIM)).astype(F32),
        ln1=row(ln1), w_in=w_in.astype(BF16),
        qg=row(q_norm * (LOG2_E * HEAD_DIM ** -0.5)),
        kg=row(k_norm),
        w_fourier=w_fourier.astype(BF16), attn_out_norm=row(attn_out_norm), fourier_out_norm=row(fourier_out_norm),
        w_out=w_out.astype(BF16), ln2=row(ln2),
        wg_dense=w_gate_dense.astype(BF16), wu_dense=w_up_dense.astype(BF16), wd_dense=w_down_dense.astype(BF16),
        router=jnp.stack([router_hi, router_lo], axis=1),
        wg_exp=w_gate_exp.astype(BF16), wu_exp=w_up_exp.astype(BF16), wd_exp=w_down_exp.astype(BF16),
    )


def kernel(x_prompt, x_sample, ln1, w_in, q_norm, k_norm, w_fourier, attn_out_norm, fourier_out_norm, w_out, ln2,
           w_gate_dense, w_up_dense, w_down_dense, router, w_gate_exp, w_up_exp, w_down_exp):
    wts = _prepare_weights(ln1, w_in, q_norm, k_norm, w_fourier, attn_out_norm, fourier_out_norm, w_out, ln2,
                           w_gate_dense, w_up_dense, w_down_dense, router, w_gate_exp, w_up_exp, w_down_exp)
    return _trunk(x_prompt, wts), _trunk(x_sample, wts)
```

```python
import functools
import math

import numpy as np
import jax
import jax.numpy as jnp
from jax import lax
from jax.experimental import pallas as pl
from jax.experimental.pallas import tpu as pltpu

F32 = jnp.float32
BF16 = jnp.bfloat16

HEAD_DIM = 128
N_Q_HEADS = 8
N_KV_HEADS = 2
Q_GROUP = N_Q_HEADS // N_KV_HEADS
ATTN_WIDTH = N_Q_HEADS * HEAD_DIM
KV_WIDTH = N_KV_HEADS * HEAD_DIM
GROUP_DIM = 128
N_FOURIER_GROUPS = 8
FOURIER_WIDTH = N_FOURIER_GROUPS * GROUP_DIM
N_EXPERTS = 8
GRID_W = 64
ROPE_THETA = 10000.0
ROPE_PAIRS = HEAD_DIM // 4
EPS = 1e-6
LOG2_E = 1.4426950408889634
MAX_UNSHIFTED_SCORE = 48.0
FAST_UNROLL = 8
FOURIER_BLOCK = 16
LANES = 128
VMEM_LIMIT = 56 * 1024 * 1024


def _params(*sem):
    return pltpu.CompilerParams(dimension_semantics=sem, vmem_limit_bytes=VMEM_LIMIT)


def _rms(x, g):
    return x * lax.rsqrt(jnp.mean(x * x, axis=-1, keepdims=True) + EPS) * g


def _const_spec(shape):
    zeros = (0,) * len(shape)
    return pl.BlockSpec(shape, lambda *_: zeros, pipeline_mode=pl.Buffered(1))


def _in_proj_kernel(x_ref, ln_ref, w_ref, qg_ref, kg_ref, vs_ref, cos_ref, sin_ref, q_ref, k_ref, v_ref, u_ref):
    h = _rms(x_ref[...], ln_ref[...]).astype(BF16)
    proj = jnp.dot(h, w_ref[...], preferred_element_type=F32)
    cos = cos_ref[...]
    sin = sin_ref[...]
    lane = lax.broadcasted_iota(jnp.int32, cos.shape, 1)
    lower = (lane & (2 * ROPE_PAIRS - 1)) < ROPE_PAIRS

    def norm_rope(t, g):
        y = _rms(t, g)
        partner = jnp.where(lower, pltpu.roll(y, HEAD_DIM - ROPE_PAIRS, 1), pltpu.roll(y, ROPE_PAIRS, 1))
        return y * cos + partner * sin

    qg = qg_ref[...]
    kg = kg_ref[...]
    for hd in range(N_Q_HEADS):
        sl = slice(hd * HEAD_DIM, (hd + 1) * HEAD_DIM)
        q_ref[:, sl] = norm_rope(proj[:, sl], qg).astype(q_ref.dtype)
    for hd in range(N_KV_HEADS):
        sl = slice(hd * HEAD_DIM, (hd + 1) * HEAD_DIM)
        k_ref[:, sl] = norm_rope(proj[:, ATTN_WIDTH + hd * HEAD_DIM:ATTN_WIDTH + (hd + 1) * HEAD_DIM], kg).astype(k_ref.dtype)
    vs = vs_ref[...]
    ones = jnp.broadcast_to(vs, (proj.shape[0], HEAD_DIM)).astype(v_ref.dtype)
    for hd in range(N_KV_HEADS):
        v0 = ATTN_WIDTH + KV_WIDTH + hd * HEAD_DIM
        v_ref[:, 2 * hd * HEAD_DIM:(2 * hd + 1) * HEAD_DIM] = (proj[:, v0:v0 + HEAD_DIM] * vs).astype(v_ref.dtype)
        v_ref[:, (2 * hd + 1) * HEAD_DIM:(2 * hd + 2) * HEAD_DIM] = ones
    u_ref[...] = proj[:, ATTN_WIDTH + 2 * KV_WIDTH:].astype(u_ref.dtype)


def _in_proj(x, ln, w, qg, kg, vs, cos, sin, seq, tm):
    t, d = x.shape
    in_width = w.shape[1]
    blocks_per_seq = seq // tm
    row = lambda i: (i, 0)
    pos = lambda i: (i % blocks_per_seq, 0)
    return pl.pallas_call(
        _in_proj_kernel,
        grid=(t // tm,),
        in_specs=[
            pl.BlockSpec((tm, d), row),
            _const_spec((1, d)),
            _const_spec((d, in_width)),
            _const_spec((1, HEAD_DIM)),
            _const_spec((1, HEAD_DIM)),
            _const_spec((1, HEAD_DIM)),
            pl.BlockSpec((tm, HEAD_DIM), pos),
            pl.BlockSpec((tm, HEAD_DIM), pos),
        ],
        out_specs=[
            pl.BlockSpec((tm, ATTN_WIDTH), row),
            pl.BlockSpec((tm, KV_WIDTH), row),
            pl.BlockSpec((tm, 2 * KV_WIDTH), row),
            pl.BlockSpec((tm, FOURIER_WIDTH), row),
        ],
        out_shape=[
            jax.ShapeDtypeStruct((t, ATTN_WIDTH), BF16),
            jax.ShapeDtypeStruct((t, KV_WIDTH), BF16),
            jax.ShapeDtypeStruct((t, 2 * KV_WIDTH), BF16),
            jax.ShapeDtypeStruct((t, FOURIER_WIDTH), BF16),
        ],
        compiler_params=_params("parallel"),
        name="in_proj",
    )(x, ln, w, qg, kg, vs, cos, sin)


def _attn_kernel(bounded_ref, q_ref, k_ref, v_ref, o_ref, *, tk):
    tq = q_ref.shape[0]
    seq = k_ref.shape[0]
    q = jnp.concatenate([q_ref[:, g * HEAD_DIM:(g + 1) * HEAD_DIM] for g in range(Q_GROUP)], axis=0)
    rows = q.shape[0]

    def scores(j):
        start = pl.multiple_of(j * tk, tk)
        kj = k_ref[pl.ds(start, tk), :]
        vj = v_ref[pl.ds(start, tk), :]
        return lax.dot_general(q, kj, (((1,), (1,)), ((), ())), preferred_element_type=F32), vj

    def write(out):
        for g in range(Q_GROUP):
            o_ref[:, g * HEAD_DIM:(g + 1) * HEAD_DIM] = out[g * tq:(g + 1) * tq].astype(o_ref.dtype)

    @pl.when(bounded_ref[0] == 1)
    def _():
        def body(j, acc):
            s, vj = scores(j)
            return acc + jnp.dot(jnp.exp2(s).astype(BF16), vj, preferred_element_type=F32)

        acc = lax.fori_loop(0, seq // tk, body, jnp.zeros((rows, 2 * HEAD_DIM), F32), unroll=FAST_UNROLL)
        write(acc[:, :HEAD_DIM] / acc[:, HEAD_DIM:])

    @pl.when(bounded_ref[0] == 0)
    def _():
        def body(j, carry):
            m, acc = carry
            s, vj = scores(j)
            m_new = jnp.maximum(m, jnp.max(s, axis=-1, keepdims=True))
            p = jnp.exp2(s - m_new).astype(BF16)
            return m_new, jnp.exp2(m - m_new) * acc + jnp.dot(p, vj, preferred_element_type=F32)

        m0 = jnp.full((rows, 1), -jnp.inf, F32)
        _, acc = lax.fori_loop(0, seq // tk, body, (m0, jnp.zeros((rows, 2 * HEAD_DIM), F32)))
        write(acc[:, :HEAD_DIM] / acc[:, HEAD_DIM:])


def _attention(bounded, q, k, v1, batch, seq, tq, tk):
    t = q.shape[0]
    qb = seq // tq
    gw = Q_GROUP * HEAD_DIM
    return pl.pallas_call(
        functools.partial(_attn_kernel, tk=tk),
        grid_spec=pltpu.PrefetchScalarGridSpec(
            num_scalar_prefetch=1,
            grid=(batch, N_KV_HEADS, qb),
            in_specs=[
                pl.BlockSpec((tq, gw), lambda b, h, i, f: (b * qb + i, h)),
                pl.BlockSpec((seq, HEAD_DIM), lambda b, h, i, f: (b, h)),
                pl.BlockSpec((seq, 2 * HEAD_DIM), lambda b, h, i, f: (b, h)),
            ],
            out_specs=pl.BlockSpec((tq, gw), lambda b, h, i, f: (b * qb + i, h)),
        ),
        out_shape=jax.ShapeDtypeStruct((t, ATTN_WIDTH), BF16),
        compiler_params=_params("parallel", "parallel", "parallel"),
        name="attention",
    )(bounded, q, k, v1)


def _group(g):
    return slice(g * GROUP_DIM, (g + 1) * GROUP_DIM)


def _fourier1_kernel(u_ref, cmat_ref, dft_ref, twr_ref, twi_ref, are_ref, aim_ref, s_re, s_im):
    n1, fb, _ = u_ref.shape
    for g in range(N_FOURIER_GROUPS):
        s_re[g] = u_ref[:, :, _group(g)].astype(F32).reshape(n1 * fb, GROUP_DIM)
    lane = lax.broadcasted_iota(jnp.int32, twr_ref.shape, 1)
    for j in range(fb):
        rows = pl.ds(j, n1, stride=fb)
        zre, zim = [], []
        for g in range(N_FOURIER_GROUPS):
            r = jnp.dot(s_re[g, rows, :].astype(BF16), cmat_ref[...], preferred_element_type=F32)
            zre.append(r[:, :GROUP_DIM])
            zim.append(r[:, GROUP_DIM:])
        z = jnp.concatenate([jnp.concatenate(zre, axis=1), jnp.concatenate(zim, axis=1)], axis=0).astype(BF16)
        d = jnp.dot(dft_ref[...], z, preferred_element_type=F32)
        dre, dim = d[:n1], d[n1:]
        pick = lane == pl.program_id(1) * fb + j
        tr = jnp.sum(jnp.where(pick, twr_ref[...], 0.0), axis=1, keepdims=True)
        ti = jnp.sum(jnp.where(pick, twi_ref[...], 0.0), axis=1, keepdims=True)
        are = tr * dre - ti * dim
        aim = tr * dim + ti * dre
        for g in range(N_FOURIER_GROUPS):
            s_re[g, rows, :] = are[:, _group(g)]
            s_im[g, rows, :] = aim[:, _group(g)]
    for g in range(N_FOURIER_GROUPS):
        are_ref[:, :, _group(g)] = s_re[g].reshape(n1, fb, GROUP_DIM).astype(are_ref.dtype)
        aim_ref[:, :, _group(g)] = s_im[g].reshape(n1, fb, GROUP_DIM).astype(aim_ref.dtype)


def _fourier2_kernel(are_ref, aim_ref, dft_ref, wf_ref, g_ref, o_ref, s_out):
    fb, n2, width = are_ref.shape
    gain = g_ref[...]
    for k in range(fb):
        a = jnp.concatenate([are_ref[k], aim_ref[k]], axis=0)
        f = jnp.dot(dft_ref[...], a, preferred_element_type=F32).astype(BF16)
        ys = [jnp.dot(f[:, _group(g)], wf_ref[g], preferred_element_type=F32) for g in range(N_FOURIER_GROUPS)]
        ms = sum(jnp.sum(y * y, axis=-1, keepdims=True) for y in ys) * (1.0 / width)
        scale = lax.rsqrt(ms + EPS)
        for g in range(N_FOURIER_GROUPS):
            s_out[g, pl.ds(k, n2, stride=fb), :] = ys[g] * scale * gain[:, _group(g)]
    for g in range(N_FOURIER_GROUPS):
        o_ref[:, :, _group(g)] = s_out[g].reshape(n2, fb, GROUP_DIM).astype(o_ref.dtype)


def _fourier_tables(seq):
    n1 = 1 << (int(math.log2(seq)) // 2)
    n2 = seq // n1
    c = np.arange(GROUP_DIM)
    ang = 2 * np.pi * np.outer(c, c) / GROUP_DIM
    cmat = np.concatenate([np.cos(ang), -np.sin(ang)], axis=1) / math.sqrt(GROUP_DIM)
    a1 = 2 * np.pi * np.outer(np.arange(n1), np.arange(n1)) / n1
    d1 = np.block([[np.cos(a1), np.sin(a1)], [-np.sin(a1), np.cos(a1)]]) / math.sqrt(n1)
    psi = 2 * np.pi * np.outer(np.arange(n1), np.arange(n2)) / seq
    pad = (-n2) % LANES
    twr = np.pad(np.cos(psi), ((0, 0), (0, pad)))
    twi = np.pad(-np.sin(psi), ((0, 0), (0, pad)))
    a2 = 2 * np.pi * np.outer(np.arange(n2), np.arange(n2)) / n2
    d2 = np.concatenate([np.cos(a2), np.sin(a2)], axis=1) / math.sqrt(n2)
    return (n1, n2, jnp.asarray(cmat, BF16), jnp.asarray(d1, BF16), jnp.asarray(twr, F32),
            jnp.asarray(twi, F32), jnp.asarray(d2, BF16))


def _fourier(u, w_f, gain, batch, seq):
    t, width = u.shape
    n1, n2, cmat, d1, twr, twi, d2 = _fourier_tables(seq)
    fb = FOURIER_BLOCK
    assert n1 % fb == 0 and n2 % fb == 0
    u3 = u.reshape(batch * n1, n2, width)
    are, aim = pl.pallas_call(
        _fourier1_kernel,
        grid=(batch, n2 // fb),
        in_specs=[
            pl.BlockSpec((n1, fb, width), lambda b, j: (b, j, 0)),
            _const_spec(cmat.shape),
            _const_spec(d1.shape),
            _const_spec(twr.shape),
            _const_spec(twi.shape),
        ],
        out_specs=[pl.BlockSpec((n1, fb, width), lambda b, j: (b, j, 0))] * 2,
        out_shape=[jax.ShapeDtypeStruct((batch * n1, n2, width), BF16)] * 2,
        scratch_shapes=[pltpu.VMEM((N_FOURIER_GROUPS, n1 * fb, GROUP_DIM), F32)] * 2,
        compiler_params=_params("parallel", "parallel"),
        name="fourier_stage1",
    )(u3, cmat, d1, twr, twi)
    out = pl.pallas_call(
        _fourier2_kernel,
        grid=(batch, n1 // fb),
        in_specs=[
            pl.BlockSpec((fb, n2, width), lambda b, j: (b * (n1 // fb) + j, 0, 0)),
            pl.BlockSpec((fb, n2, width), lambda b, j: (b * (n1 // fb) + j, 0, 0)),
            _const_spec(d2.shape),
            _const_spec(w_f.shape),
            _const_spec(gain.shape),
        ],
        out_specs=pl.BlockSpec((n2, fb, width), lambda b, j: (b, j, 0)),
        out_shape=jax.ShapeDtypeStruct((batch * n2, n1, width), BF16),
        scratch_shapes=[pltpu.VMEM((N_FOURIER_GROUPS, n2 * fb, GROUP_DIM), F32)],
        compiler_params=_params("parallel", "parallel"),
        name="fourier_stage2",
    )(are, aim, d2, w_f, gain)
    return out.reshape(t, width)


def _out_proj_kernel(a_ref, f_ref, x_ref, ag_ref, w_ref, ln2_ref, *rest, routed):
    if routed:
        router_ref, x1_ref, h2_ref, sel_ref, gate_ref = rest
    else:
        x1_ref, h2_ref = rest
    an = _rms(a_ref[...].astype(F32), ag_ref[...]).astype(BF16)
    y = jnp.dot(an, w_ref[:ATTN_WIDTH, :], preferred_element_type=F32)
    y = y + jnp.dot(f_ref[...], w_ref[ATTN_WIDTH:, :], preferred_element_type=F32)
    x1 = x_ref[...] + y
    x1_ref[...] = x1
    h2 = _rms(x1, ln2_ref[...])
    h2_ref[...] = h2.astype(h2_ref.dtype)
    if routed:
        h_hi = h2.astype(BF16)
        h_lo = (h2 - h_hi.astype(F32)).astype(BF16)
        logits = (jnp.dot(h_hi, router_ref[0], preferred_element_type=F32)
                  + jnp.dot(h_lo, router_ref[0], preferred_element_type=F32)
                  + jnp.dot(h_hi, router_ref[1], preferred_element_type=F32))
        lane = lax.broadcasted_iota(jnp.int32, logits.shape, 1)
        lg = jnp.where(lane < N_EXPERTS, logits, -jnp.inf)
        m1 = jnp.max(lg, axis=-1, keepdims=True)
        i1 = jnp.min(jnp.where(lg == m1, lane, LANES), axis=-1, keepdims=True)
        lg2 = jnp.where(lane == i1, -jnp.inf, lg)
        m2 = jnp.max(lg2, axis=-1, keepdims=True)
        i2 = jnp.min(jnp.where(lg2 == m2, lane, LANES), axis=-1, keepdims=True)
        e2 = jnp.exp(m2 - m1)
        g1 = 1.0 / (1.0 + e2)
        g2 = e2 * g1
        sel_ref[...] = jnp.where(lane == 0, i1, jnp.where(lane == 1, i2, 0))
        gate_ref[...] = jnp.where(lane == 0, g1, jnp.where(lane == 1, g2, 0.0))


def _out_proj(a, f, x, ag, w, ln2, router, tm):
    t, d = x.shape
    routed = router is not None
    row = lambda i: (i, 0)
    in_specs = [
        pl.BlockSpec((tm, ATTN_WIDTH), row),
        pl.BlockSpec((tm, FOURIER_WIDTH), row),
        pl.BlockSpec((tm, d), row),
        _const_spec((1, ATTN_WIDTH)),
        _const_spec(w.shape),
        _const_spec((1, d)),
    ]
    out_specs = [pl.BlockSpec((tm, d), row), pl.BlockSpec((tm, d), row)]
    out_shape = [jax.ShapeDtypeStruct((t, d), F32), jax.ShapeDtypeStruct((t, d), F32 if routed else BF16)]
    args = [a, f, x, ag, w, ln2]
    if routed:
        in_specs.append(_const_spec(router.shape))
        out_specs += [pl.BlockSpec((tm, LANES), row)] * 2
        out_shape += [jax.ShapeDtypeStruct((t, LANES), jnp.int32), jax.ShapeDtypeStruct((t, LANES), F32)]
        args.append(router)
    return pl.pallas_call(
        functools.partial(_out_proj_kernel, routed=routed),
        grid=(t // tm,),
        in_specs=in_specs,
        out_specs=out_specs,
        out_shape=out_shape,
        compiler_params=_params("parallel"),
        name="out_proj_routed" if routed else "out_proj",
    )(*args)


def _swiglu_accumulate(h_ref, wg_ref, wu_ref, wd_ref, acc_ref):
    h = h_ref[...]
    g = jnp.dot(h, wg_ref[...], preferred_element_type=F32)
    u = jnp.dot(h, wu_ref[...], preferred_element_type=F32)
    act = (g * jax.nn.sigmoid(g) * u).astype(BF16)
    acc_ref[...] += jnp.dot(act, wd_ref[...], preferred_element_type=F32)


def _ffn_dense_kernel(h_ref, wg_ref, wu_ref, wd_ref, x_ref, o_ref, acc_ref):
    j = pl.program_id(1)

    @pl.when(j == 0)
    def _():
        acc_ref[...] = jnp.zeros_like(acc_ref)

    _swiglu_accumulate(h_ref, wg_ref, wu_ref, wd_ref, acc_ref)

    @pl.when(j == pl.num_programs(1) - 1)
    def _():
        o_ref[...] = x_ref[...] + acc_ref[...]


def _ffn_dense(h, wg, wu, wd, x, tm, tf):
    r, d = h.shape
    d_ff = wg.shape[1]
    return pl.pallas_call(
        _ffn_dense_kernel,
        grid=(r // tm, d_ff // tf),
        in_specs=[
            pl.BlockSpec((tm, d), lambda i, j: (i, 0)),
            pl.BlockSpec((d, tf), lambda i, j: (0, j)),
            pl.BlockSpec((d, tf), lambda i, j: (0, j)),
            pl.BlockSpec((tf, d), lambda i, j: (j, 0)),
            pl.BlockSpec((tm, d), lambda i, j: (i, 0)),
        ],
        out_specs=pl.BlockSpec((tm, d), lambda i, j: (i, 0)),
        out_shape=jax.ShapeDtypeStruct((r, d), F32),
        scratch_shapes=[pltpu.VMEM((tm, d), F32)],
        compiler_params=_params("parallel", "arbitrary"),
        name="ffn_dense",
    )(h, wg, wu, wd, x)


def _ffn_experts_kernel(te_ref, na_ref, src_ref, h_ref, wg_ref, wu_ref, wd_ref, o_ref,
                        rows_ref, hb_ref, acc_ref, sem, *, nj):
    i = pl.program_id(0)
    j = pl.program_id(1)
    tm = hb_ref.shape[0]
    n_active = na_ref[0]
    slot = lax.rem(i, 2)
    per_step = -(-tm // nj)
    every_step = tm - per_step * (nj - 1)

    def row_copy(tile, r, s):
        return pltpu.make_async_copy(h_ref.at[pl.ds(src_ref[tile * tm + r], 1)], rows_ref.at[s, pl.ds(r, 1)],
                                     sem.at[s])

    @pl.when(j == 0)
    def _():
        @pl.when(i == 0)
        def _():
            def start(r, c):
                row_copy(0, r, 0).start()
                return c
            lax.fori_loop(0, tm, start, 0)

        @pl.when(i <= n_active)
        def _():
            pltpu.make_async_copy(rows_ref.at[slot], rows_ref.at[slot], sem.at[slot]).wait()

    @pl.when(i < n_active)
    def _():
        @pl.when(j == 0)
        def _():
            hb_ref[...] = rows_ref[slot].astype(BF16)
            acc_ref[...] = jnp.zeros_like(acc_ref)

        base = j * per_step

        @pl.when(j < nj - 1)
        def _():
            for r in range(every_step, per_step):
                row_copy(i + 1, base + r, 1 - slot).start()

        for r in range(every_step):
            row_copy(i + 1, base + r, 1 - slot).start()
        _swiglu_accumulate(hb_ref, wg_ref.at[0], wu_ref.at[0], wd_ref.at[0], acc_ref)

        @pl.when(j == nj - 1)
        def _():
            o_ref[...] = acc_ref[...]

    @pl.when(jnp.logical_and(i >= n_active, j == nj - 1))
    def _():
        o_ref[...] = jnp.zeros_like(o_ref)


def _ffn_experts(h, src, wg, wu, wd, tile_expert, n_active, tm, tf):
    d = h.shape[1]
    d_ff = wg.shape[2]
    nj = d_ff // tf
    n_tiles = src.shape[0] // tm

    def jj(i, j, na):
        return jnp.where(i < na[0], j, nj - 1)

    return pl.pallas_call(
        functools.partial(_ffn_experts_kernel, nj=nj),
        grid_spec=pltpu.PrefetchScalarGridSpec(
            num_scalar_prefetch=3,
            grid=(n_tiles, nj),
            in_specs=[
                pl.BlockSpec(memory_space=pl.ANY),
                pl.BlockSpec((1, d, tf), lambda i, j, te, na, src: (te[i], 0, jj(i, j, na))),
                pl.BlockSpec((1, d, tf), lambda i, j, te, na, src: (te[i], 0, jj(i, j, na))),
                pl.BlockSpec((1, tf, d), lambda i, j, te, na, src: (te[i], jj(i, j, na), 0)),
            ],
            out_specs=pl.BlockSpec((tm, d), lambda i, j, te, na, src: (i, 0)),
            scratch_shapes=[pltpu.VMEM((2, tm, d), F32), pltpu.VMEM((tm, d), BF16), pltpu.VMEM((tm, d), F32),
                            pltpu.SemaphoreType.DMA((2,))],
        ),
        out_shape=jax.ShapeDtypeStruct((n_tiles * tm, d), F32),
        compiler_params=_params("arbitrary", "arbitrary"),
        name="ffn_experts",
    )(tile_expert, n_active, src, h, wg, wu, wd)


def _combine_kernel(pos_ref, y_ref, x_ref, gate_ref, o_ref, buf_ref, sem):
    tc = x_ref.shape[0]
    base = pl.program_id(0) * tc

    def row_copy(r, k):
        return pltpu.make_async_copy(y_ref.at[pl.ds(pos_ref[2 * (base + r) + k], 1)],
                                     buf_ref.at[k, pl.ds(r, 1)], sem)

    def start(r, c):
        row_copy(r, 0).start()
        row_copy(r, 1).start()
        return c

    lax.fori_loop(0, tc, start, 0, unroll=8)
    pltpu.make_async_copy(buf_ref, buf_ref, sem).wait()
    gate = gate_ref[...]
    o_ref[...] = x_ref[...] + gate[:, 0:1] * buf_ref[0] + gate[:, 1:2] * buf_ref[1]


def _combine(y, pos, x, gate, tc):
    t, d = x.shape
    return pl.pallas_call(
        _combine_kernel,
        grid_spec=pltpu.PrefetchScalarGridSpec(
            num_scalar_prefetch=1,
            grid=(t // tc,),
            in_specs=[
                pl.BlockSpec(memory_space=pl.ANY),
                pl.BlockSpec((tc, d), lambda i, pos: (i, 0)),
                pl.BlockSpec((tc, LANES), lambda i, pos: (i, 0)),
            ],
            out_specs=pl.BlockSpec((tc, d), lambda i, pos: (i, 0)),
            scratch_shapes=[pltpu.VMEM((2, tc, d), F32), pltpu.SemaphoreType.DMA(())],
        ),
        out_shape=jax.ShapeDtypeStruct((t, d), F32),
        compiler_params=_params("arbitrary"),
        name="combine",
    )(pos, y, x, gate)


def _routing_plan(sel, tm):
    t = sel.shape[0]
    experts = sel[:, :2].reshape(-1)
    onehot = (experts[:, None] == jnp.arange(N_EXPERTS, dtype=jnp.int32)[None, :]).astype(jnp.int32)
    csum = jnp.cumsum(onehot, axis=0)
    rank = jnp.sum((csum - onehot) * onehot, axis=1)
    counts = csum[-1]
    padded = ((counts + tm - 1) // tm) * tm
    ends = jnp.cumsum(padded)
    starts = ends - padded
    pos = (starts[experts] + rank).astype(jnp.int32)
    n_tiles = (2 * t) // tm + N_EXPERTS + 1
    src =jnp.zeros((n_tiles * tm,), jnp.int32).at[pos].set(jnp.arange(2 * t, dtype=jnp.int32) // 2)
    tile_start = jnp.arange(n_tiles, dtype=jnp.int32) * tm
    tile_expert = jnp.sum((tile_start[:, None] >= ends[None, :]).astype(jnp.int32), axis=1)
    n_active = (ends[-1] // tm).astype(jnp.int32).reshape(1)
    last_expert = jnp.max(jnp.where(counts > 0, jnp.arange(N_EXPERTS, dtype=jnp.int32), 0))
    tile_expert = jnp.minimum(tile_expert, last_expert).astype(jnp.int32)
    return pos, src, tile_expert, n_active


def _rope_tables(seq):
    t = np.arange(seq)
    inv_freq = ROPE_THETA ** (-np.arange(ROPE_PAIRS, dtype=np.float64) / ROPE_PAIRS)
    row = (t // GRID_W)[:, None] * inv_freq
    col = (t % GRID_W)[:, None] * inv_freq
    cos = np.concatenate([np.cos(row), np.cos(row), np.cos(col), np.cos(col)], axis=1)
    sin = np.concatenate([-np.sin(row), np.sin(row), -np.sin(col), np.sin(col)], axis=1)
    return jnp.asarray(cos, F32), jnp.asarray(sin, F32)


def _tiles(seq):
    tm = min(512, seq)
    return dict(tm=tm, tq=min(512, seq), tk=min(512, seq), tf=512, te=min(512, seq), tc=min(256, seq))


def _trunk(x3, wts):
    batch, seq, d = x3.shape
    t = batch * seq
    x = x3.reshape(t, d)
    cfg = _tiles(seq)
    cos, sin = _rope_tables(seq)
    depth = wts["ln1"].shape[0]
    for l in range(depth):
        q, k, v1, u = _in_proj(x, wts["ln1"][l], wts["w_in"][l], wts["qg"][l], wts["kg"][l], wts["v_scale"][l],
                              cos, sin, seq, cfg["tm"])
        a = _attention(wts["bounded"][l], q, k, v1, batch, seq, cfg["tq"], cfg["tk"])
        f = _fourier(u, wts["w_fourier"][l], wts["fourier_out_norm"][l], batch, seq)
        i = l // 2
        if l % 2 == 0:
            x1, h2 = _out_proj(a, f, x, wts["attn_out_norm"][l], wts["w_out"][l], wts["ln2"][l], None, cfg["tm"])
            x = _ffn_dense(h2, wts["wg_dense"][i], wts["wu_dense"][i], wts["wd_dense"][i], x1, cfg["te"], cfg["tf"])
        else:
            x1, h2, sel, gate = _out_proj(a, f, x, wts["attn_out_norm"][l], wts["w_out"][l], wts["ln2"][l],
                                          wts["router"][i], cfg["tm"])
            pos, src, tile_expert, n_active = _routing_plan(sel, cfg["te"])
            ys = _ffn_experts(h2, src, wts["wg_exp"][i], wts["wu_exp"][i], wts["wd_exp"][i], tile_expert, n_active,
                              cfg["te"], cfg["tf"])
            x = _combine(ys, pos, x1, gate, cfg["tc"])
    return x.reshape(batch, seq, d)


def _prepare_weights(ln1, w_in, q_norm, k_norm, w_fourier, attn_out_norm, fourier_out_norm, w_out, ln2,
                     w_gate_dense, w_up_dense, w_down_dense, router, w_gate_exp, w_up_exp, w_down_exp):
    depth = ln1.shape[0]
    row = lambda g: g.reshape(depth, 1, -1)
    router_padded = jnp.pad(router, ((0, 0), (0, 0), (0, LANES - router.shape[-1])))
    router_hi = router_padded.astype(BF16)
    router_lo = (router_padded - router_hi.astype(F32)).astype(BF16)
    score_bound = (LOG2_E * math.sqrt(HEAD_DIM) * (1 + 2.0 ** -6)
                   * jnp.max(jnp.abs(q_norm), axis=1) * jnp.max(jnp.abs(k_norm), axis=1))
    bounded = score_bound <= MAX_UNSHIFTED_SCORE
    v_scale = jnp.where(bounded, jnp.exp2(-jnp.ceil(score_bound)), 1.0)
    return dict(
        bounded=bounded.astype(jnp.int32).reshape(depth, 1),
        v_scale=jnp.broadcast_to(v_scale.reshape(depth, 1, 1), (depth, 1, HEAD_DIM)).astype(F32),
        ln1=row(ln1), w_in=w_in.astype(BF16),
        qg=row(q_norm * (LOG2_E * HEAD_DIM ** -0.5)),
        kg=row(k_norm),
        w_fourier=w_fourier.astype(BF16), attn_out_norm=row(attn_out_norm), fourier_out_norm=row(fourier_out_norm),
        w_out=w_out.astype(BF16), ln2=row(ln2),
        wg_dense=w_gate_dense.astype(BF16), wu_dense=w_up_dense.astype(BF16), wd_dense=w_down_dense.astype(BF16),
        router=jnp.stack([router_hi, router_lo], axis=1),
        wg_exp=w_gate_exp.astype(BF16), wu_exp=w_up_exp.astype(BF16), wd_exp=w_down_exp.astype(BF16),
    )


def kernel(x_prompt, x_sample, ln1, w_in, q_norm, k_norm, w_fourier, attn_out_norm, fourier_out_norm, w_out, ln2,
           w_gate_dense, w_up_dense, w_down_dense, router, w_gate_exp, w_up_exp, w_down_exp):
    wts = _prepare_weights(ln1, w_in, q_norm, k_norm, w_fourier, attn_out_norm, fourier_out_norm, w_out, ln2,
                           w_gate_dense, w_up_dense, w_down_dense, router, w_gate_exp, w_up_exp, w_down_exp)
    return _trunk(x_prompt, wts), _trunk(x_sample, wts)
```

```python
import functools
import math

import numpy as np
import jax
import jax.numpy as jnp
from jax import lax
from jax.experimental import pallas as pl
from jax.experimental.pallas import tpu as pltpu

F32 = jnp.float32
BF16 = jnp.bfloat16

HEAD_DIM = 128
N_Q_HEADS = 8
N_KV_HEADS = 2
Q_GROUP = N_Q_HEADS // N_KV_HEADS
ATTN_WIDTH = N_Q_HEADS * HEAD_DIM
KV_WIDTH = N_KV_HEADS * HEAD_DIM
GROUP_DIM = 128
N_FOURIER_GROUPS = 8
FOURIER_WIDTH = N_FOURIER_GROUPS * GROUP_DIM
N_EXPERTS = 8
GRID_W = 64
ROPE_THETA = 10000.0
ROPE_PAIRS = HEAD_DIM // 4
EPS = 1e-6
LOG2_E = 1.4426950408889634
MAX_UNSHIFTED_SCORE = 48.0
FAST_UNROLL = 8
FOURIER_BLOCK = 16
LANES = 128
VMEM_LIMIT = 56 * 1024 * 1024


def _params(*sem):
    return pltpu.CompilerParams(dimension_semantics=sem, vmem_limit_bytes=VMEM_LIMIT)


def _rms(x, g):
    return x * lax.rsqrt(jnp.mean(x * x, axis=-1, keepdims=True) + EPS) * g


def _const_spec(shape):
    zeros = (0,) * len(shape)
    return pl.BlockSpec(shape, lambda *_: zeros, pipeline_mode=pl.Buffered(1))


def _in_proj_kernel(x_ref, ln_ref, w_ref, qg_ref, kg_ref, vs_ref, cos_ref, sin_ref, q_ref, k_ref, v_ref, u_ref):
    h = _rms(x_ref[...], ln_ref[...]).astype(BF16)
    proj = jnp.dot(h, w_ref[...], preferred_element_type=F32)
    cos = cos_ref[...]
    sin = sin_ref[...]
    lane = lax.broadcasted_iota(jnp.int32, cos.shape, 1)
    lower = (lane & (2 * ROPE_PAIRS - 1)) < ROPE_PAIRS

    def norm_rope(t, g):
        y = _rms(t, g)
        partner = jnp.where(lower, pltpu.roll(y, HEAD_DIM - ROPE_PAIRS, 1), pltpu.roll(y, ROPE_PAIRS, 1))
        return y * cos + partner * sin

    qg = qg_ref[...]
    kg = kg_ref[...]
    for hd in range(N_Q_HEADS):
        sl = slice(hd * HEAD_DIM, (hd + 1) * HEAD_DIM)
        q_ref[:, sl] = norm_rope(proj[:, sl], qg).astype(q_ref.dtype)
    for hd in range(N_KV_HEADS):
        sl = slice(hd * HEAD_DIM, (hd + 1) * HEAD_DIM)
        k_ref[:, sl] = norm_rope(proj[:, ATTN_WIDTH + hd * HEAD_DIM:ATTN_WIDTH + (hd + 1) * HEAD_DIM], kg).astype(k_ref.dtype)
    vs = vs_ref[...]
    ones = jnp.broadcast_to(vs, (proj.shape[0], HEAD_DIM)).astype(v_ref.dtype)
    for hd in range(N_KV_HEADS):
        v0 = ATTN_WIDTH + KV_WIDTH + hd * HEAD_DIM
        v_ref[:, 2 * hd * HEAD_DIM:(2 * hd + 1) * HEAD_DIM] = (proj[:, v0:v0 + HEAD_DIM] * vs).astype(v_ref.dtype)
        v_ref[:, (2 * hd + 1) * HEAD_DIM:(2 * hd + 2) * HEAD_DIM] = ones
    u_ref[...] = proj[:, ATTN_WIDTH + 2 * KV_WIDTH:].astype(u_ref.dtype)


def _in_proj(x, ln, w, qg, kg, vs, cos, sin, seq, tm):
    t, d = x.shape
    in_width = w.shape[1]
    blocks_per_seq = seq // tm
    row = lambda i: (i, 0)
    pos = lambda i: (i % blocks_per_seq, 0)
    return pl.pallas_call(
        _in_proj_kernel,
        grid=(t // tm,),
        in_specs=[
            pl.BlockSpec((tm, d), row),
            _const_spec((1, d)),
            _const_spec((d, in_width)),
            _const_spec((1, HEAD_DIM)),
            _const_spec((1, HEAD_DIM)),
            _const_spec((1, HEAD_DIM)),
            pl.BlockSpec((tm, HEAD_DIM), pos),
            pl.BlockSpec((tm, HEAD_DIM), pos),
        ],
        out_specs=[
            pl.BlockSpec((tm, ATTN_WIDTH), row),
            pl.BlockSpec((tm, KV_WIDTH), row),
            pl.BlockSpec((tm, 2 * KV_WIDTH), row),
            pl.BlockSpec((tm, FOURIER_WIDTH), row),
        ],
        out_shape=[
            jax.ShapeDtypeStruct((t, ATTN_WIDTH), BF16),
            jax.ShapeDtypeStruct((t, KV_WIDTH), BF16),
            jax.ShapeDtypeStruct((t, 2 * KV_WIDTH), BF16),
            jax.ShapeDtypeStruct((t, FOURIER_WIDTH), BF16),
        ],
        compiler_params=_params("parallel"),
        name="in_proj",
    )(x, ln, w, qg, kg, vs, cos, sin)


def _attn_kernel(bounded_ref, q_ref, k_ref, v_ref, o_ref, *, tk):
    tq = q_ref.shape[0]
    seq = k_ref.shape[0]
    q = jnp.concatenate([q_ref[:, g * HEAD_DIM:(g + 1) * HEAD_DIM] for g in range(Q_GROUP)], axis=0)
    rows = q.shape[0]

    def scores(j):
        start = pl.multiple_of(j * tk, tk)
        kj = k_ref[pl.ds(start, tk), :]
        vj = v_ref[pl.ds(start, tk), :]
        return lax.dot_general(q, kj, (((1,), (1,)), ((), ())), preferred_element_type=F32), vj

    def write(out):
        for g in range(Q_GROUP):
            o_ref[:, g * HEAD_DIM:(g + 1) * HEAD_DIM] = out[g * tq:(g + 1) * tq].astype(o_ref.dtype)

    @pl.when(bounded_ref[0] == 1)
    def _():
        def body(j, acc):
            s, vj = scores(j)
            return acc + jnp.dot(jnp.exp2(s).astype(BF16), vj, preferred_element_type=F32)

        acc = lax.fori_loop(0, seq // tk, body, jnp.zeros((rows, 2 * HEAD_DIM), F32), unroll=FAST_UNROLL)
        write(acc[:, :HEAD_DIM] / acc[:, HEAD_DIM:])

    @pl.when(bounded_ref[0] == 0)
    def _():
        def body(j, carry):
            m, acc = carry
            s, vj = scores(j)
            m_new = jnp.maximum(m, jnp.max(s, axis=-1, keepdims=True))
            p = jnp.exp2(s - m_new).astype(BF16)
            return m_new, jnp.exp2(m - m_new) * acc + jnp.dot(p, vj, preferred_element_type=F32)

        m0 = jnp.full((rows, 1), -jnp.inf, F32)
        _, acc = lax.fori_loop(0, seq // tk, body, (m0, jnp.zeros((rows, 2 * HEAD_DIM), F32)))
        write(acc[:, :HEAD_DIM] / acc[:, HEAD_DIM:])


def _attention(bounded, q, k, v1, batch, seq, tq, tk):
    t = q.shape[0]
    qb = seq // tq
    gw = Q_GROUP * HEAD_DIM
    return pl.pallas_call(
        functools.partial(_attn_kernel, tk=tk),
        grid_spec=pltpu.PrefetchScalarGridSpec(
            num_scalar_prefetch=1,
            grid=(batch, N_KV_HEADS, qb),
            in_specs=[
                pl.BlockSpec((tq, gw), lambda b, h, i, f: (b * qb + i, h)),
                pl.BlockSpec((seq, HEAD_DIM), lambda b, h, i, f: (b, h)),
                pl.BlockSpec((seq, 2 * HEAD_DIM), lambda b, h, i, f: (b, h)),
            ],
            out_specs=pl.BlockSpec((tq, gw), lambda b, h, i, f: (b * qb + i, h)),
        ),
        out_shape=jax.ShapeDtypeStruct((t, ATTN_WIDTH), BF16),
        compiler_params=_params("parallel", "parallel", "parallel"),
        name="attention",
    )(bounded, q, k, v1)


def _group(g):
    return slice(g * GROUP_DIM, (g + 1) * GROUP_DIM)


def _fourier1_kernel(u_ref, cmat_ref, dft_ref, twr_ref, twi_ref, are_ref, aim_ref, s_re, s_im):
    n1, fb, _ = u_ref.shape
    for g in range(N_FOURIER_GROUPS):
        s_re[g] = u_ref[:, :, _group(g)].astype(F32).reshape(n1 * fb, GROUP_DIM)
    lane = lax.broadcasted_iota(jnp.int32, twr_ref.shape, 1)
    for j in range(fb):
        rows = pl.ds(j, n1, stride=fb)
        zre, zim = [], []
        for g in range(N_FOURIER_GROUPS):
            r = jnp.dot(s_re[g, rows, :].astype(BF16), cmat_ref[...], preferred_element_type=F32)
            zre.append(r[:, :GROUP_DIM])
            zim.append(r[:, GROUP_DIM:])
        z = jnp.concatenate([jnp.concatenate(zre, axis=1), jnp.concatenate(zim, axis=1)], axis=0).astype(BF16)
        d = jnp.dot(dft_ref[...], z, preferred_element_type=F32)
        dre, dim = d[:n1], d[n1:]
        pick = lane == pl.program_id(1) * fb + j
        tr = jnp.sum(jnp.where(pick, twr_ref[...], 0.0), axis=1, keepdims=True)
        ti = jnp.sum(jnp.where(pick, twi_ref[...], 0.0), axis=1, keepdims=True)
        are = tr * dre - ti * dim
        aim = tr * dim + ti * dre
        for g in range(N_FOURIER_GROUPS):
            s_re[g, rows, :] = are[:, _group(g)]
            s_im[g, rows, :] = aim[:, _group(g)]
    for g in range(N_FOURIER_GROUPS):
        are_ref[:, :, _group(g)] = s_re[g].reshape(n1, fb, GROUP_DIM).astype(are_ref.dtype)
        aim_ref[:, :, _group(g)] = s_im[g].reshape(n1, fb, GROUP_DIM).astype(aim_ref.dtype)


def _fourier2_kernel(are_ref, aim_ref, dft_ref, wf_ref, g_ref, o_ref, s_out):
    fb, n2, width = are_ref.shape
    gain = g_ref[...]
    for k in range(fb):
        a = jnp.concatenate([are_ref[k], aim_ref[k]], axis=0)
        f = jnp.dot(dft_ref[...], a, preferred_element_type=F32).astype(BF16)
        ys = [jnp.dot(f[:, _group(g)], wf_ref[g], preferred_element_type=F32) for g in range(N_FOURIER_GROUPS)]
        ms = sum(jnp.sum(y * y, axis=-1, keepdims=True) for y in ys) * (1.0 / width)
        scale = lax.rsqrt(ms + EPS)
        for g in range(N_FOURIER_GROUPS):
            s_out[g, pl.ds(k, n2, stride=fb), :] = ys[g] * scale * gain[:, _group(g)]
    for g in range(N_FOURIER_GROUPS):
        o_ref[:, :, _group(g)] = s_out[g].reshape(n2, fb, GROUP_DIM).astype(o_ref.dtype)


def _fourier_tables(seq):
    n1 = 1 << (int(math.log2(seq)) // 2)
    n2 = seq // n1
    c = np.arange(GROUP_DIM)
    ang = 2 * np.pi * np.outer(c, c) / GROUP_DIM
    cmat = np.concatenate([np.cos(ang), -np.sin(ang)], axis=1) / math.sqrt(GROUP_DIM)
    a1 = 2 * np.pi * np.outer(np.arange(n1), np.arange(n1)) / n1
    d1 = np.block([[np.cos(a1), np.sin(a1)], [-np.sin(a1), np.cos(a1)]]) / math.sqrt(n1)
    psi = 2 * np.pi * np.outer(np.arange(n1), np.arange(n2)) / seq
    pad = (-n2) % LANES
    twr = np.pad(np.cos(psi), ((0, 0), (0, pad)))
    twi = np.pad(-np.sin(psi), ((0, 0), (0, pad)))
    a2 = 2 * np.pi * np.outer(np.arange(n2), np.arange(n2)) / n2
    d2 = np.concatenate([np.cos(a2), np.sin(a2)], axis=1) / math.sqrt(n2)
    return (n1, n2, jnp.asarray(cmat, BF16), jnp.asarray(d1, BF16), jnp.asarray(twr, F32),
            jnp.asarray(twi, F32), jnp.asarray(d2, BF16))


def _fourier(u, w_f, gain, batch, seq):
    t, width = u.shape
    n1, n2, cmat, d1, twr, twi, d2 = _fourier_tables(seq)
    fb = FOURIER_BLOCK
    assert n1 % fb == 0 and n2 % fb == 0
    u3 = u.reshape(batch * n1, n2, width)
    are, aim = pl.pallas_call(
        _fourier1_kernel,
        grid=(batch, n2 // fb),
        in_specs=[
            pl.BlockSpec((n1, fb, width), lambda b, j: (b, j, 0)),
            _const_spec(cmat.shape),
            _const_spec(d1.shape),
            _const_spec(twr.shape),
            _const_spec(twi.shape),
        ],
        out_specs=[pl.BlockSpec((n1, fb, width), lambda b, j: (b, j, 0))] * 2,
        out_shape=[jax.ShapeDtypeStruct((batch * n1, n2, width), BF16)] * 2,
        scratch_shapes=[pltpu.VMEM((N_FOURIER_GROUPS, n1 * fb, GROUP_DIM), F32)] * 2,
        compiler_params=_params("parallel", "parallel"),
        name="fourier_stage1",
    )(u3, cmat, d1, twr, twi)
    out = pl.pallas_call(
        _fourier2_kernel,
        grid=(batch, n1 // fb),
        in_specs=[
            pl.BlockSpec((fb, n2, width), lambda b, j: (b * (n1 // fb) + j, 0, 0)),
            pl.BlockSpec((fb, n2, width), lambda b, j: (b * (n1 // fb) + j, 0, 0)),
            _const_spec(d2.shape),
            _const_spec(w_f.shape),
            _const_spec(gain.shape),
        ],
        out_specs=pl.BlockSpec((n2, fb, width), lambda b, j: (b, j, 0)),
        out_shape=jax.ShapeDtypeStruct((batch * n2, n1, width), BF16),
        scratch_shapes=[pltpu.VMEM((N_FOURIER_GROUPS, n2 * fb, GROUP_DIM), F32)],
        compiler_params=_params("parallel", "parallel"),
        name="fourier_stage2",
    )(are, aim, d2, w_f, gain)
    return out.reshape(t, width)


def _out_proj_kernel(a_ref, f_ref, x_ref, ag_ref, w_ref, ln2_ref, *rest, routed):
    if routed:
        router_ref, x1_ref, h2_ref, sel_ref, gate_ref = rest
    else:
        x1_ref, h2_ref = rest
    tm = x_ref.shape[0]
    for part in range(2):
        rows = slice(part * (tm // 2), (part + 1) * (tm // 2))
        an = _rms(a_ref[rows, :].astype(F32), ag_ref[...]).astype(BF16)
        y = jnp.dot(an, w_ref[:ATTN_WIDTH, :], preferred_element_type=F32)
        y = y + jnp.dot(f_ref[rows, :], w_ref[ATTN_WIDTH:, :], preferred_element_type=F32)
        x1 = x_ref[rows, :] + y
        x1_ref[rows, :] = x1
        h2 = _rms(x1, ln2_ref[...])
        h2_ref[rows, :] = h2.astype(h2_ref.dtype)
        if routed:
            h_hi = h2.astype(BF16)
            h_lo = (h2 - h_hi.astype(F32)).astype(BF16)
            both = jnp.dot(h_hi, router_ref[...], preferred_element_type=F32)
            logits = (both[:, :LANES] + both[:, LANES:]
                      + jnp.dot(h_lo, router_ref[:, :LANES], preferred_element_type=F32))
            lane = lax.broadcasted_iota(jnp.int32, logits.shape, 1)
            lg = jnp.where(lane < N_EXPERTS, logits, -jnp.inf)
            m1 = jnp.max(lg, axis=-1, keepdims=True)
            i1 = jnp.min(jnp.where(lg == m1, lane, LANES), axis=-1, keepdims=True)
            lg2 = jnp.where(lane == i1, -jnp.inf, lg)
            m2 = jnp.max(lg2, axis=-1, keepdims=True)
            i2 = jnp.min(jnp.where(lg2 == m2, lane, LANES), axis=-1, keepdims=True)
            e2 = jnp.exp(m2 - m1)
            g1 = 1.0 / (1.0 + e2)
            g2 = e2 * g1
            sel_ref[rows, :] = jnp.where(lane == 0, i1, jnp.where(lane == 1, i2, 0))
            gate_ref[rows, :] = jnp.where(lane == 0, g1, jnp.where(lane == 1, g2, 0.0))


def _out_proj(a, f, x, ag, w, ln2, router, tm):
    t, d = x.shape
    routed = router is not None
    row = lambda i: (i, 0)
    in_specs = [
        pl.BlockSpec((tm, ATTN_WIDTH), row),
        pl.BlockSpec((tm, FOURIER_WIDTH), row),
        pl.BlockSpec((tm, d), row),
        _const_spec((1, ATTN_WIDTH)),
        _const_spec(w.shape),
        _const_spec((1, d)),
    ]
    out_specs = [pl.BlockSpec((tm, d), row), pl.BlockSpec((tm, d), row)]
    out_shape = [jax.ShapeDtypeStruct((t, d), F32), jax.ShapeDtypeStruct((t, d), F32 if routed else BF16)]
    args = [a, f, x, ag, w, ln2]
    if routed:
        in_specs.append(_const_spec(router.shape))
        out_specs += [pl.BlockSpec((tm, LANES), row)] * 2
        out_shape += [jax.ShapeDtypeStruct((t, LANES), jnp.int32), jax.ShapeDtypeStruct((t, LANES), F32)]
        args.append(router)
    return pl.pallas_call(
        functools.partial(_out_proj_kernel, routed=routed),
        grid=(t // tm,),
        in_specs=in_specs,
        out_specs=out_specs,
        out_shape=out_shape,
        compiler_params=_params("parallel"),
        name="out_proj_routed" if routed else "out_proj",
    )(*args)


def _swiglu_accumulate(h_ref, wg_ref, wu_ref, wd_ref, acc_ref):
    h = h_ref[...]
    g = jnp.dot(h, wg_ref[...], preferred_element_type=F32)
    u = jnp.dot(h, wu_ref[...], preferred_element_type=F32)
    act = (g * jax.nn.sigmoid(g) * u).astype(BF16)
    acc_ref[...] += jnp.dot(act, wd_ref[...], preferred_element_type=F32)


def _ffn_dense_kernel(h_ref, wg_ref, wu_ref, wd_ref, x_ref, o_ref, acc_ref):
    j = pl.program_id(1)

    @pl.when(j == 0)
    def _():
        acc_ref[...] = jnp.zeros_like(acc_ref)

    _swiglu_accumulate(h_ref, wg_ref, wu_ref, wd_ref, acc_ref)

    @pl.when(j == pl.num_programs(1) - 1)
    def _():
        o_ref[...] = x_ref[...] + acc_ref[...]


def _ffn_dense(h, wg, wu, wd, x, tm, tf):
    r, d = h.shape
    d_ff = wg.shape[1]
    return pl.pallas_call(
        _ffn_dense_kernel,
        grid=(r // tm, d_ff // tf),
        in_specs=[
            pl.BlockSpec((tm, d), lambda i, j: (i, 0)),
            pl.BlockSpec((d, tf), lambda i, j: (0, j)),
            pl.BlockSpec((d, tf), lambda i, j: (0, j)),
            pl.BlockSpec((tf, d), lambda i, j: (j, 0)),
            pl.BlockSpec((tm, d), lambda i, j: (i, 0)),
        ],
        out_specs=pl.BlockSpec((tm, d), lambda i, j: (i, 0)),
        out_shape=jax.ShapeDtypeStruct((r, d), F32),
        scratch_shapes=[pltpu.VMEM((tm, d), F32)],
        compiler_params=_params("parallel", "arbitrary"),
        name="ffn_dense",
    )(h, wg, wu, wd, x)


def _ffn_experts_kernel(te_ref, na_ref, src_ref, h_ref, wg_ref, wu_ref, wd_ref, o_ref,
                        rows_ref, hb_ref, acc_ref, sem, *, nj):
    i = pl.program_id(0)
    j = pl.program_id(1)
    tm = hb_ref.shape[0]
    n_active = na_ref[0]
    slot = lax.rem(i, 2)
    per_step = -(-tm // nj)
    every_step = tm - per_step * (nj - 1)

    def row_copy(tile, r, s):
        return pltpu.make_async_copy(h_ref.at[pl.ds(src_ref[tile * tm + r], 1)], rows_ref.at[s, pl.ds(r, 1)],
                                     sem.at[s])

    @pl.when(j == 0)
    def _():
        @pl.when(i == 0)
        def _():
            def start(r, c):
                row_copy(0, r, 0).start()
                return c
            lax.fori_loop(0, tm, start, 0)

        @pl.when(i <= n_active)
        def _():
            pltpu.make_async_copy(rows_ref.at[slot], rows_ref.at[slot], sem.at[slot]).wait()

    @pl.when(i < n_active)
    def _():
        @pl.when(j == 0)
        def _():
            hb_ref[...] = rows_ref[slot].astype(BF16)
            acc_ref[...] = jnp.zeros_like(acc_ref)

        base = j * per_step

        @pl.when(j < nj - 1)
        def _():
            for r in range(every_step, per_step):
                row_copy(i + 1, base + r, 1 - slot).start()

        for r in range(every_step):
            row_copy(i + 1, base + r, 1 - slot).start()
        _swiglu_accumulate(hb_ref, wg_ref.at[0], wu_ref.at[0], wd_ref.at[0], acc_ref)

        @pl.when(j == nj - 1)
        def _():
            o_ref[...] = acc_ref[...]

    @pl.when(jnp.logical_and(i >= n_active, j == nj - 1))
    def _():
        o_ref[...] = jnp.zeros_like(o_ref)


def _ffn_experts(h, src, wg, wu, wd, tile_expert, n_active, tm, tf):
    d = h.shape[1]
    d_ff = wg.shape[2]
    nj = d_ff // tf
    n_tiles = src.shape[0] // tm

    def jj(i, j, na):
        return jnp.where(i < na[0], j, nj - 1)

    return pl.pallas_call(
        functools.partial(_ffn_experts_kernel, nj=nj),
        grid_spec=pltpu.PrefetchScalarGridSpec(
            num_scalar_prefetch=3,
            grid=(n_tiles, nj),
            in_specs=[
                pl.BlockSpec(memory_space=pl.ANY),
                pl.BlockSpec((1, d, tf), lambda i, j, te, na, src: (te[i], 0, jj(i, j, na))),
                pl.BlockSpec((1, d, tf), lambda i, j, te, na, src: (te[i], 0, jj(i, j, na))),
                pl.BlockSpec((1, tf, d), lambda i, j, te, na, src: (te[i], jj(i, j, na), 0)),
            ],
            out_specs=pl.BlockSpec((tm, d), lambda i, j, te, na, src: (i, 0)),
            scratch_shapes=[pltpu.VMEM((2, tm, d), F32), pltpu.VMEM((tm, d), BF16), pltpu.VMEM((tm, d), F32),
                            pltpu.SemaphoreType.DMA((2,))],
        ),
        out_shape=jax.ShapeDtypeStruct((n_tiles * tm, d), F32),
        compiler_params=_params("arbitrary", "arbitrary"),
        name="ffn_experts",
    )(tile_expert, n_active, src, h, wg, wu, wd)


def _combine_kernel(pos_ref, y_ref, x_ref, gate_ref, o_ref, buf_ref, sem):
    tc = x_ref.shape[0]
    base = pl.program_id(0) * tc

    def row_copy(r, k):
        return pltpu.make_async_copy(y_ref.at[pl.ds(pos_ref[2 * (base + r) + k], 1)],
                                     buf_ref.at[k, pl.ds(r, 1)], sem)

    def start(r, c):
        row_copy(r, 0).start(priority=0)
        row_copy(r, 1).start(priority=1)
        return c

    lax.fori_loop(0, tc, start, 0, unroll=8)
    pltpu.make_async_copy(buf_ref, buf_ref, sem).wait()
    gate = gate_ref[...]
    o_ref[...] = x_ref[...] + gate[:, 0:1] * buf_ref[0] + gate[:, 1:2] * buf_ref[1]


def _combine(y, pos, x, gate, tc):
    t, d = x.shape
    return pl.pallas_call(
        _combine_kernel,
        grid_spec=pltpu.PrefetchScalarGridSpec(
            num_scalar_prefetch=1,
            grid=(t // tc,),
            in_specs=[
                pl.BlockSpec(memory_space=pl.ANY),
                pl.BlockSpec((tc, d), lambda i, pos: (i, 0)),
                pl.BlockSpec((tc, LANES), lambda i, pos: (i, 0)),
            ],
            out_specs=pl.BlockSpec((tc, d), lambda i, pos: (i, 0)),
            scratch_shapes=[pltpu.VMEM((2, tc, d), F32), pltpu.SemaphoreType.DMA(())],
        ),
        out_shape=jax.ShapeDtypeStruct((t, d), F32),
        compiler_params=_params("arbitrary"),
        name="combine",
    )(pos, y, x, gate)


def _routing_plan(sel, tm):
    t = sel.shape[0]
    experts = sel[:, :2].reshape(-1)
    onehot = (experts[:, None] == jnp.arange(N_EXPERTS, dtype=jnp.int32)[None, :]).astype(jnp.int32)
    csum = jnp.cumsum(onehot, axis=0)
    rank = jnp.sum((csum - onehot) * onehot, axis=1)
    counts = csum[-1]
    padded = ((counts + tm - 1) // tm) * tm
    ends = jnp.cumsum(padded)
    starts = ends - padded
    pos = (starts[experts] + rank).astype(jnp.int32)
    n_tiles = (2 * t) // tm + N_EXPERTS + 1
    src =jnp.zeros((n_tiles * tm,), jnp.int32).at[pos].set(jnp.arange(2 * t, dtype=jnp.int32) // 2)
    tile_start = jnp.arange(n_tiles, dtype=jnp.int32) * tm
    tile_expert = jnp.sum((tile_start[:, None] >= ends[None, :]).astype(jnp.int32), axis=1)
    n_active = (ends[-1] // tm).astype(jnp.int32).reshape(1)
    last_expert = jnp.max(jnp.where(counts > 0, jnp.arange(N_EXPERTS, dtype=jnp.int32), 0))
    tile_expert = jnp.minimum(tile_expert, last_expert).astype(jnp.int32)
    return pos, src, tile_expert, n_active


def _rope_tables(seq):
    t = np.arange(seq)
    inv_freq = ROPE_THETA ** (-np.arange(ROPE_PAIRS, dtype=np.float64) / ROPE_PAIRS)
    row = (t // GRID_W)[:, None] * inv_freq
    col = (t % GRID_W)[:, None] * inv_freq
    cos = np.concatenate([np.cos(row), np.cos(row), np.cos(col), np.cos(col)], axis=1)
    sin = np.concatenate([-np.sin(row), np.sin(row), -np.sin(col), np.sin(col)], axis=1)
    return jnp.asarray(cos, F32), jnp.asarray(sin, F32)


def _tiles(seq):
    tm = min(512, seq)
    return dict(tm=tm, tq=min(512, seq), tk=min(512, seq), tf=512, te=min(512, seq), tc=min(256, seq))


def _trunk(x3, wts):
    batch, seq, d = x3.shape
    t = batch * seq
    x = x3.reshape(t, d)
    cfg = _tiles(seq)
    cos, sin = _rope_tables(seq)
    depth = wts["ln1"].shape[0]
    for l in range(depth):
        q, k, v1, u = _in_proj(x, wts["ln1"][l], wts["w_in"][l], wts["qg"][l], wts["kg"][l], wts["v_scale"][l],
                              cos, sin, seq, cfg["tm"])
        a = _attention(wts["bounded"][l], q, k, v1, batch, seq, cfg["tq"], cfg["tk"])
        f = _fourier(u, wts["w_fourier"][l], wts["fourier_out_norm"][l], batch, seq)
        i = l // 2
        if l % 2 == 0:
            x1, h2 = _out_proj(a, f, x, wts["attn_out_norm"][l], wts["w_out"][l], wts["ln2"][l], None, cfg["tm"])
            x = _ffn_dense(h2, wts["wg_dense"][i], wts["wu_dense"][i], wts["wd_dense"][i], x1, cfg["te"], cfg["tf"])
        else:
            x1, h2, sel, gate = _out_proj(a, f, x, wts["attn_out_norm"][l], wts["w_out"][l], wts["ln2"][l],
                                          wts["router"][i], cfg["tm"])
            pos, src, tile_expert, n_active = _routing_plan(sel, cfg["te"])
            ys = _ffn_experts(h2, src, wts["wg_exp"][i], wts["wu_exp"][i], wts["wd_exp"][i], tile_expert, n_active,
                              cfg["te"], cfg["tf"])
            x = _combine(ys, pos, x1, gate, cfg["tc"])
    return x.reshape(batch, seq, d)


def _prepare_weights(ln1, w_in, q_norm, k_norm, w_fourier, attn_out_norm, fourier_out_norm, w_out, ln2,
                     w_gate_dense, w_up_dense, w_down_dense, router, w_gate_exp, w_up_exp, w_down_exp):
    depth = ln1.shape[0]
    row = lambda g: g.reshape(depth, 1, -1)
    router_padded = jnp.pad(router, ((0, 0), (0, 0), (0, LANES - router.shape[-1])))
    router_hi = router_padded.astype(BF16)
    router_lo = (router_padded - router_hi.astype(F32)).astype(BF16)
    score_bound = (LOG2_E * math.sqrt(HEAD_DIM) * (1 + 2.0 ** -6)
                   * jnp.max(jnp.abs(q_norm), axis=1) * jnp.max(jnp.abs(k_norm), axis=1))
    bounded = score_bound <= MAX_UNSHIFTED_SCORE
    v_scale = jnp.where(bounded, jnp.exp2(-jnp.ceil(score_bound)), 1.0)
    return dict(
        bounded=bounded.astype(jnp.int32).reshape(depth, 1),
        v_scale=jnp.broadcast_to(v_scale.reshape(depth, 1, 1), (depth, 1, HEAD_DIM)).astype(F32),
        ln1=row(ln1), w_in=w_in.astype(BF16),
        qg=row(q_norm * (LOG2_E * HEAD_DIM ** -0.5)),
        kg=row(k_norm),
        w_fourier=w_fourier.astype(BF16), attn_out_norm=row(attn_out_norm), fourier_out_norm=row(fourier_out_norm),
        w_out=w_out.astype(BF16), ln2=row(ln2),
        wg_dense=w_gate_dense.astype(BF16), wu_dense=w_up_dense.astype(BF16), wd_dense=w_down_dense.astype(BF16),
        router=jnp.concatenate([router_hi, router_lo], axis=2),
        wg_exp=w_gate_exp.astype(BF16), wu_exp=w_up_exp.astype(BF16), wd_exp=w_down_exp.astype(BF16),
    )


def kernel(x_prompt, x_sample, ln1, w_in, q_norm, k_norm, w_fourier, attn_out_norm, fourier_out_norm, w_out, ln2,
           w_gate_dense, w_up_dense, w_down_dense, router, w_gate_exp, w_up_exp, w_down_exp):
    wts = _prepare_weights(ln1, w_in, q_norm, k_norm, w_fourier, attn_out_norm, fourier_out_norm, w_out, ln2,
                           w_gate_dense, w_up_dense, w_down_dense, router, w_gate_exp, w_up_exp, w_down_exp)
    return _trunk(x_prompt, wts), _trunk(x_sample, wts)
```

```python
import functools
import math

import numpy as np
import jax
import jax.numpy as jnp
from jax import lax
from jax.experimental import pallas as pl
from jax.experimental.pallas import tpu as pltpu

F32 = jnp.float32
BF16 = jnp.bfloat16

HEAD_DIM = 128
N_Q_HEADS = 8
N_KV_HEADS = 2
Q_GROUP = N_Q_HEADS // N_KV_HEADS
ATTN_WIDTH = N_Q_HEADS * HEAD_DIM
KV_WIDTH = N_KV_HEADS * HEAD_DIM
GROUP_DIM = 128
N_FOURIER_GROUPS = 8
FOURIER_WIDTH = N_FOURIER_GROUPS * GROUP_DIM
N_EXPERTS = 8
GRID_W = 64
ROPE_THETA = 10000.0
ROPE_PAIRS = HEAD_DIM // 4
EPS = 1e-6
LOG2_E = 1.4426950408889634
MAX_UNSHIFTED_SCORE = 48.0
FAST_UNROLL = 8
FOURIER_BLOCK = 16
LANES = 128
VMEM_LIMIT = 56 * 1024 * 1024


def _params(*sem):
    return pltpu.CompilerParams(dimension_semantics=sem, vmem_limit_bytes=VMEM_LIMIT)


def _rms(x, g):
    return x * lax.rsqrt(jnp.mean(x * x, axis=-1, keepdims=True) + EPS) * g


def _const_spec(shape):
    zeros = (0,) * len(shape)
    return pl.BlockSpec(shape, lambda *_: zeros, pipeline_mode=pl.Buffered(1))


def _in_proj_kernel(x_ref, ln_ref, w_ref, qg_ref, kg_ref, vs_ref, cos_ref, sin_ref, q_ref, k_ref, v_ref, u_ref):
    h = _rms(x_ref[...], ln_ref[...]).astype(BF16)
    proj = jnp.dot(h, w_ref[...], preferred_element_type=F32)
    cos = cos_ref[...]
    sin = sin_ref[...]
    lane = lax.broadcasted_iota(jnp.int32, cos.shape, 1)
    lower = (lane & (2 * ROPE_PAIRS - 1)) < ROPE_PAIRS

    def norm_rope(t, g):
        y = _rms(t, g)
        partner = jnp.where(lower, pltpu.roll(y, HEAD_DIM - ROPE_PAIRS, 1), pltpu.roll(y, ROPE_PAIRS, 1))
        return y * cos + partner * sin

    qg = qg_ref[...]
    kg = kg_ref[...]
    for hd in range(N_Q_HEADS):
        sl = slice(hd * HEAD_DIM, (hd + 1) * HEAD_DIM)
        q_ref[:, sl] = norm_rope(proj[:, sl], qg).astype(q_ref.dtype)
    for hd in range(N_KV_HEADS):
        sl = slice(hd * HEAD_DIM, (hd + 1) * HEAD_DIM)
        k_ref[:, sl] = norm_rope(proj[:, ATTN_WIDTH + hd * HEAD_DIM:ATTN_WIDTH + (hd + 1) * HEAD_DIM], kg).astype(k_ref.dtype)
    vs = vs_ref[...]
    ones = jnp.broadcast_to(vs, (proj.shape[0], HEAD_DIM)).astype(v_ref.dtype)
    for hd in range(N_KV_HEADS):
        v0 = ATTN_WIDTH + KV_WIDTH + hd * HEAD_DIM
        v_ref[:, 2 * hd * HEAD_DIM:(2 * hd + 1) * HEAD_DIM] = (proj[:, v0:v0 + HEAD_DIM] * vs).astype(v_ref.dtype)
        v_ref[:, (2 * hd + 1) * HEAD_DIM:(2 * hd + 2) * HEAD_DIM] = ones
    u_ref[...] = proj[:, ATTN_WIDTH + 2 * KV_WIDTH:].astype(u_ref.dtype)


def _in_proj(x, ln, w, qg, kg, vs, cos, sin, seq, tm):
    t, d = x.shape
    in_width = w.shape[1]
    blocks_per_seq = seq // tm
    row = lambda i: (i, 0)
    pos = lambda i: (i % blocks_per_seq, 0)
    return pl.pallas_call(
        _in_proj_kernel,
        grid=(t // tm,),
        in_specs=[
            pl.BlockSpec((tm, d), row),
            _const_spec((1, d)),
            _const_spec((d, in_width)),
            _const_spec((1, HEAD_DIM)),
            _const_spec((1, HEAD_DIM)),
            _const_spec((1, HEAD_DIM)),
            pl.BlockSpec((tm, HEAD_DIM), pos),
            pl.BlockSpec((tm, HEAD_DIM), pos),
        ],
        out_specs=[
            pl.BlockSpec((tm, ATTN_WIDTH), row),
            pl.BlockSpec((tm, KV_WIDTH), row),
            pl.BlockSpec((tm, 2 * KV_WIDTH), row),
            pl.BlockSpec((tm, FOURIER_WIDTH), row),
        ],
        out_shape=[
            jax.ShapeDtypeStruct((t, ATTN_WIDTH), BF16),
            jax.ShapeDtypeStruct((t, KV_WIDTH), BF16),
            jax.ShapeDtypeStruct((t, 2 * KV_WIDTH), BF16),
            jax.ShapeDtypeStruct((t, FOURIER_WIDTH), BF16),
        ],
        compiler_params=_params("parallel"),
        name="in_proj",
    )(x, ln, w, qg, kg, vs, cos, sin)


def _attn_kernel(bounded_ref, q_ref, k_ref, v_ref, *rest, tk, n_cast):
    o_ref = rest[n_cast]
    for src, dst in zip(rest[:n_cast], rest[n_cast + 1:]):
        dst[...] = src[...].astype(dst.dtype)
    tq = q_ref.shape[0]
    seq = k_ref.shape[0]
    q = jnp.concatenate([q_ref[:, g * HEAD_DIM:(g + 1) * HEAD_DIM] for g in range(Q_GROUP)], axis=0)
    rows = q.shape[0]

    def scores(j):
        start = pl.multiple_of(j * tk, tk)
        kj = k_ref[pl.ds(start, tk), :]
        vj = v_ref[pl.ds(start, tk), :]
        return lax.dot_general(q, kj, (((1,), (1,)), ((), ())), preferred_element_type=F32), vj

    def write(out):
        for g in range(Q_GROUP):
            o_ref[:, g * HEAD_DIM:(g + 1) * HEAD_DIM] = out[g * tq:(g + 1) * tq].astype(o_ref.dtype)

    @pl.when(bounded_ref[0] == 1)
    def _():
        def body(j, acc):
            s, vj = scores(j)
            return acc + jnp.dot(jnp.exp2(s).astype(BF16), vj, preferred_element_type=F32)

        acc = lax.fori_loop(0, seq // tk, body, jnp.zeros((rows, 2 * HEAD_DIM), F32), unroll=FAST_UNROLL)
        write(acc[:, :HEAD_DIM] / acc[:, HEAD_DIM:])

    @pl.when(bounded_ref[0] == 0)
    def _():
        def body(j, carry):
            m, acc = carry
            s, vj = scores(j)
            m_new = jnp.maximum(m, jnp.max(s, axis=-1, keepdims=True))
            p = jnp.exp2(s - m_new).astype(BF16)
            return m_new, jnp.exp2(m - m_new) * acc + jnp.dot(p, vj, preferred_element_type=F32)

        m0 = jnp.full((rows, 1), -jnp.inf, F32)
        _, acc = lax.fori_loop(0, seq // tk, body, (m0, jnp.zeros((rows, 2 * HEAD_DIM), F32)))
        write(acc[:, :HEAD_DIM] / acc[:, HEAD_DIM:])


def _cast_chunks(batch, seq, tq, weights, n_experts):
    steps = batch * N_KV_HEADS * (seq // tq)
    if steps % n_experts:
        return None
    chunks = steps // n_experts
    if any(w.shape[1] % chunks or (w.shape[1] // chunks) % 16 for w in weights):
        return None
    return chunks


def _attention(bounded, q, k, v1, batch, seq, tq, tk, cast=None):
    t = q.shape[0]
    qb = seq // tq
    gw = Q_GROUP * HEAD_DIM
    in_specs = [
        pl.BlockSpec((tq, gw), lambda b, h, i, f: (b * qb + i, h)),
        pl.BlockSpec((seq, HEAD_DIM), lambda b, h, i, f: (b, h)),
        pl.BlockSpec((seq, 2 * HEAD_DIM), lambda b, h, i, f: (b, h)),
    ]
    out_specs = [pl.BlockSpec((tq, gw), lambda b, h, i, f: (b * qb + i, h))]
    out_shape = [jax.ShapeDtypeStruct((t, ATTN_WIDTH), BF16)]
    args = [bounded, q, k, v1]
    n_cast = 0
    if cast is not None:
        chunks = _cast_chunks(batch, seq, tq, cast, cast[0].shape[0])

        def chunk_index(b, h, i, f):
            step = (b * N_KV_HEADS + h) * qb + i
            return step // chunks, step % chunks, 0

        for w in cast:
            block = (1, w.shape[1] // chunks, w.shape[2])
            in_specs.append(pl.BlockSpec(block, chunk_index))
            out_specs.append(pl.BlockSpec(block, chunk_index))
            out_shape.append(jax.ShapeDtypeStruct(w.shape, BF16))
        args += list(cast)
        n_cast = len(cast)
    outs = pl.pallas_call(
        functools.partial(_attn_kernel, tk=tk, n_cast=n_cast),
        grid_spec=pltpu.PrefetchScalarGridSpec(
            num_scalar_prefetch=1,
            grid=(batch, N_KV_HEADS, qb),
            in_specs=in_specs,
            out_specs=out_specs,
        ),
        out_shape=out_shape,
        compiler_params=_params("parallel", "parallel", "parallel"),
        name="attention_cast" if n_cast else "attention",
    )(*args)
    return outs[0], tuple(outs[1:])


def _group(g):
    return slice(g * GROUP_DIM, (g + 1) * GROUP_DIM)


def _fourier1_kernel(u_ref, cmat_ref, dft_ref, twr_ref, twi_ref, are_ref, aim_ref, s_re, s_im):
    n1, fb, _ = u_ref.shape
    for g in range(N_FOURIER_GROUPS):
        s_re[g] = u_ref[:, :, _group(g)].astype(F32).reshape(n1 * fb, GROUP_DIM)
    lane = lax.broadcasted_iota(jnp.int32, twr_ref.shape, 1)
    for j in range(fb):
        rows = pl.ds(j, n1, stride=fb)
        zre, zim = [], []
        for g in range(N_FOURIER_GROUPS):
            r = jnp.dot(s_re[g, rows, :].astype(BF16), cmat_ref[...], preferred_element_type=F32)
            zre.append(r[:, :GROUP_DIM])
            zim.append(r[:, GROUP_DIM:])
        z = jnp.concatenate([jnp.concatenate(zre, axis=1), jnp.concatenate(zim, axis=1)], axis=0).astype(BF16)
        d = jnp.dot(dft_ref[...], z, preferred_element_type=F32)
        dre, dim = d[:n1], d[n1:]
        pick = lane == pl.program_id(1) * fb + j
        tr = jnp.sum(jnp.where(pick, twr_ref[...], 0.0), axis=1, keepdims=True)
        ti = jnp.sum(jnp.where(pick, twi_ref[...], 0.0), axis=1, keepdims=True)
        are = tr * dre - ti * dim
        aim = tr * dim + ti * dre
        for g in range(N_FOURIER_GROUPS):
            s_re[g, rows, :] = are[:, _group(g)]
            s_im[g, rows, :] = aim[:, _group(g)]
    for g in range(N_FOURIER_GROUPS):
        are_ref[:, :, _group(g)] = s_re[g].reshape(n1, fb, GROUP_DIM).astype(are_ref.dtype)
        aim_ref[:, :, _group(g)] = s_im[g].reshape(n1, fb, GROUP_DIM).astype(aim_ref.dtype)


def _fourier2_kernel(are_ref, aim_ref, dft_ref, wf_ref, g_ref, o_ref, s_out):
    fb, n2, width = are_ref.shape
    gain = g_ref[...]
    for k in range(fb):
        a = jnp.concatenate([are_ref[k], aim_ref[k]], axis=0)
        f = jnp.dot(dft_ref[...], a, preferred_element_type=F32).astype(BF16)
        ys = [jnp.dot(f[:, _group(g)], wf_ref[g], preferred_element_type=F32) for g in range(N_FOURIER_GROUPS)]
        ms = sum(jnp.sum(y * y, axis=-1, keepdims=True) for y in ys) * (1.0 / width)
        scale = lax.rsqrt(ms + EPS)
        for g in range(N_FOURIER_GROUPS):
            s_out[g, pl.ds(k, n2, stride=fb), :] = ys[g] * scale * gain[:, _group(g)]
    for g in range(N_FOURIER_GROUPS):
        o_ref[:, :, _group(g)] = s_out[g].reshape(n2, fb, GROUP_DIM).astype(o_ref.dtype)


def _fourier_tables(seq):
    n1 = 1 << (int(math.log2(seq)) // 2)
    n2 = seq // n1
    c = np.arange(GROUP_DIM)
    ang = 2 * np.pi * np.outer(c, c) / GROUP_DIM
    cmat = np.concatenate([np.cos(ang), -np.sin(ang)], axis=1) / math.sqrt(GROUP_DIM)
    a1 = 2 * np.pi * np.outer(np.arange(n1), np.arange(n1)) / n1
    d1 = np.block([[np.cos(a1), np.sin(a1)], [-np.sin(a1), np.cos(a1)]]) / math.sqrt(n1)
    psi = 2 * np.pi * np.outer(np.arange(n1), np.arange(n2)) / seq
    pad = (-n2) % LANES
    twr = np.pad(np.cos(psi), ((0, 0), (0, pad)))
    twi = np.pad(-np.sin(psi), ((0, 0), (0, pad)))
    a2 = 2 * np.pi * np.outer(np.arange(n2), np.arange(n2)) / n2
    d2 = np.concatenate([np.cos(a2), np.sin(a2)], axis=1) / math.sqrt(n2)
    return (n1, n2, jnp.asarray(cmat, BF16), jnp.asarray(d1, BF16), jnp.asarray(twr, F32),
            jnp.asarray(twi, F32), jnp.asarray(d2, BF16))


def _fourier(u, w_f, gain, batch, seq):
    t, width = u.shape
    n1, n2, cmat, d1, twr, twi, d2 = _fourier_tables(seq)
    fb = FOURIER_BLOCK
    assert n1 % fb == 0 and n2 % fb == 0
    u3 = u.reshape(batch * n1, n2, width)
    are, aim = pl.pallas_call(
        _fourier1_kernel,
        grid=(batch, n2 // fb),
        in_specs=[
            pl.BlockSpec((n1, fb, width), lambda b, j: (b, j, 0)),
            _const_spec(cmat.shape),
            _const_spec(d1.shape),
            _const_spec(twr.shape),
            _const_spec(twi.shape),
        ],
        out_specs=[pl.BlockSpec((n1, fb, width), lambda b, j: (b, j, 0))] * 2,
        out_shape=[jax.ShapeDtypeStruct((batch * n1, n2, width), BF16)] * 2,
        scratch_shapes=[pltpu.VMEM((N_FOURIER_GROUPS, n1 * fb, GROUP_DIM), F32)] * 2,
        compiler_params=_params("parallel", "parallel"),
        name="fourier_stage1",
    )(u3, cmat, d1, twr, twi)
    out = pl.pallas_call(
        _fourier2_kernel,
        grid=(batch, n1 // fb),
        in_specs=[
            pl.BlockSpec((fb, n2, width), lambda b, j: (b * (n1 // fb) + j, 0, 0)),
            pl.BlockSpec((fb, n2, width), lambda b, j: (b * (n1 // fb) + j, 0, 0)),
            _const_spec(d2.shape),
            _const_spec(w_f.shape),
            _const_spec(gain.shape),
        ],
        out_specs=pl.BlockSpec((n2, fb, width), lambda b, j: (b, j, 0)),
        out_shape=jax.ShapeDtypeStruct((batch * n2, n1, width), BF16),
        scratch_shapes=[pltpu.VMEM((N_FOURIER_GROUPS, n2 * fb, GROUP_DIM), F32)],
        compiler_params=_params("parallel", "parallel"),
        name="fourier_stage2",
    )(are, aim, d2, w_f, gain)
    return out.reshape(t, width)


def _out_proj_kernel(a_ref, f_ref, x_ref, ag_ref, w_ref, ln2_ref, *rest, routed):
    if routed:
        router_ref, x1_ref, h2_ref, sel_ref, gate_ref = rest
    else:
        x1_ref, h2_ref = rest
    tm = x_ref.shape[0]
    for part in range(2):
        rows = slice(part * (tm // 2), (part + 1) * (tm // 2))
        an = _rms(a_ref[rows, :].astype(F32), ag_ref[...]).astype(BF16)
        y = jnp.dot(an, w_ref[:ATTN_WIDTH, :], preferred_element_type=F32)
        y = y + jnp.dot(f_ref[rows, :], w_ref[ATTN_WIDTH:, :], preferred_element_type=F32)
        x1 = x_ref[rows, :] + y
        x1_ref[rows, :] = x1
        h2 = _rms(x1, ln2_ref[...])
        h2_ref[rows, :] = h2.astype(h2_ref.dtype)
        if routed:
            h_hi = h2.astype(BF16)
            h_lo = (h2 - h_hi.astype(F32)).astype(BF16)
            both = jnp.dot(h_hi, router_ref[...], preferred_element_type=F32)
            logits = (both[:, :LANES] + both[:, LANES:]
                      + jnp.dot(h_lo, router_ref[:, :LANES], preferred_element_type=F32))
            lane = lax.broadcasted_iota(jnp.int32, logits.shape, 1)
            lg = jnp.where(lane < N_EXPERTS, logits, -jnp.inf)
            m1 = jnp.max(lg, axis=-1, keepdims=True)
            i1 = jnp.min(jnp.where(lg == m1, lane, LANES), axis=-1, keepdims=True)
            lg2 = jnp.where(lane == i1, -jnp.inf, lg)
            m2 = jnp.max(lg2, axis=-1, keepdims=True)
            i2 = jnp.min(jnp.where(lg2 == m2, lane, LANES), axis=-1, keepdims=True)
            e2 = jnp.exp(m2 - m1)
            g1 = 1.0 / (1.0 + e2)
            g2 = e2 * g1
            sel_ref[rows, :] = jnp.where(lane == 0, i1, jnp.where(lane == 1, i2, 0))
            gate_ref[rows, :] = jnp.where(lane == 0, g1, jnp.where(lane == 1, g2, 0.0))


def _out_proj(a, f, x, ag, w, ln2, router, tm):
    t, d = x.shape
    routed = router is not None
    row = lambda i: (i, 0)
    in_specs = [
        pl.BlockSpec((tm, ATTN_WIDTH), row),
        pl.BlockSpec((tm, FOURIER_WIDTH), row),
        pl.BlockSpec((tm, d), row),
        _const_spec((1, ATTN_WIDTH)),
        _const_spec(w.shape),
        _const_spec((1, d)),
    ]
    out_specs = [pl.BlockSpec((tm, d), row), pl.BlockSpec((tm, d), row)]
    out_shape = [jax.ShapeDtypeStruct((t, d), F32), jax.ShapeDtypeStruct((t, d), F32 if routed else BF16)]
    args = [a, f, x, ag, w, ln2]
    if routed:
        in_specs.append(_const_spec(router.shape))
        out_specs += [pl.BlockSpec((tm, LANES), row)] * 2
        out_shape += [jax.ShapeDtypeStruct((t, LANES), jnp.int32), jax.ShapeDtypeStruct((t, LANES), F32)]
        args.append(router)
    return pl.pallas_call(
        functools.partial(_out_proj_kernel, routed=routed),
        grid=(t // tm,),
        in_specs=in_specs,
        out_specs=out_specs,
        out_shape=out_shape,
        compiler_params=_params("parallel"),
        name="out_proj_routed" if routed else "out_proj",
    )(*args)


def _swiglu_accumulate(h_ref, wg_ref, wu_ref, wd_ref, acc_ref):
    h = h_ref[...]
    g = jnp.dot(h, wg_ref[...], preferred_element_type=F32)
    u = jnp.dot(h, wu_ref[...], preferred_element_type=F32)
    act = (g * jax.nn.sigmoid(g) * u).astype(BF16)
    acc_ref[...] += jnp.dot(act, wd_ref[...], preferred_element_type=F32)


def _ffn_dense_kernel(h_ref, wg_ref, wu_ref, wd_ref, x_ref, o_ref, acc_ref):
    j = pl.program_id(1)

    @pl.when(j == 0)
    def _():
        acc_ref[...] = jnp.zeros_like(acc_ref)

    _swiglu_accumulate(h_ref, wg_ref, wu_ref, wd_ref, acc_ref)

    @pl.when(j == pl.num_programs(1) - 1)
    def _():
        o_ref[...] = x_ref[...] + acc_ref[...]


def _ffn_dense(h, wg, wu, wd, x, tm, tf):
    r, d = h.shape
    d_ff = wg.shape[1]
    return pl.pallas_call(
        _ffn_dense_kernel,
        grid=(r // tm, d_ff // tf),
        in_specs=[
            pl.BlockSpec((tm, d), lambda i, j: (i, 0)),
            pl.BlockSpec((d, tf), lambda i, j: (0, j)),
            pl.BlockSpec((d, tf), lambda i, j: (0, j)),
            pl.BlockSpec((tf, d), lambda i, j: (j, 0)),
            pl.BlockSpec((tm, d), lambda i, j: (i, 0)),
        ],
        out_specs=pl.BlockSpec((tm, d), lambda i, j: (i, 0)),
        out_shape=jax.ShapeDtypeStruct((r, d), F32),
        scratch_shapes=[pltpu.VMEM((tm, d), F32)],
        compiler_params=_params("parallel", "arbitrary"),
        name="ffn_dense",
    )(h, wg, wu, wd, x)


def _ffn_experts_kernel(te_ref, na_ref, src_ref, h_ref, wg_ref, wu_ref, wd_ref, o_ref,
                        rows_ref, hb_ref, acc_ref, sem, *, nj):
    i = pl.program_id(0)
    j = pl.program_id(1)
    tm = hb_ref.shape[0]
    n_active = na_ref[0]
    slot = lax.rem(i, 2)
    per_step = -(-tm // nj)
    every_step = tm - per_step * (nj - 1)

    def row_copy(tile, r, s):
        return pltpu.make_async_copy(h_ref.at[pl.ds(src_ref[tile * tm + r], 1)], rows_ref.at[s, pl.ds(r, 1)],
                                     sem.at[s])

    @pl.when(j == 0)
    def _():
        @pl.when(i == 0)
        def _():
            def start(r, c):
                row_copy(0, r, 0).start()
                return c
            lax.fori_loop(0, tm, start, 0)

        @pl.when(i <= n_active)
        def _():
            pltpu.make_async_copy(rows_ref.at[slot], rows_ref.at[slot], sem.at[slot]).wait()

    @pl.when(i < n_active)
    def _():
        @pl.when(j == 0)
        def _():
            hb_ref[...] = rows_ref[slot].astype(BF16)
            acc_ref[...] = jnp.zeros_like(acc_ref)

        base = j * per_step

        @pl.when(j < nj - 1)
        def _():
            for r in range(every_step, per_step):
                row_copy(i + 1, base + r, 1 - slot).start()

        for r in range(every_step):
            row_copy(i + 1, base + r, 1 - slot).start()
        _swiglu_accumulate(hb_ref, wg_ref.at[0], wu_ref.at[0], wd_ref.at[0], acc_ref)

        @pl.when(j == nj - 1)
        def _():
            o_ref[...] = acc_ref[...]

    @pl.when(jnp.logical_and(i >= n_active, j == nj - 1))
    def _():
        o_ref[...] = jnp.zeros_like(o_ref)


def _ffn_experts(h, src, wg, wu, wd, tile_expert, n_active, tm, tf):
    d = h.shape[1]
    d_ff = wg.shape[2]
    nj = d_ff // tf
    n_tiles = src.shape[0] // tm

    def jj(i, j, na):
        return jnp.where(i < na[0], j, nj - 1)

    return pl.pallas_call(
        functools.partial(_ffn_experts_kernel, nj=nj),
        grid_spec=pltpu.PrefetchScalarGridSpec(
            num_scalar_prefetch=3,
            grid=(n_tiles, nj),
            in_specs=[
                pl.BlockSpec(memory_space=pl.ANY),
                pl.BlockSpec((1, d, tf), lambda i, j, te, na, src: (te[i], 0, jj(i, j, na))),
                pl.BlockSpec((1, d, tf), lambda i, j, te, na, src: (te[i], 0, jj(i, j, na))),
                pl.BlockSpec((1, tf, d), lambda i, j, te, na, src: (te[i], jj(i, j, na), 0)),
            ],
            out_specs=pl.BlockSpec((tm, d), lambda i, j, te, na, src: (i, 0)),
            scratch_shapes=[pltpu.VMEM((2, tm, d), F32), pltpu.VMEM((tm, d), BF16), pltpu.VMEM((tm, d), F32),
                            pltpu.SemaphoreType.DMA((2,))],
        ),
        out_shape=jax.ShapeDtypeStruct((n_tiles * tm, d), F32),
        compiler_params=_params("arbitrary", "arbitrary"),
        name="ffn_experts",
    )(tile_expert, n_active, src, h, wg, wu, wd)


def _combine_kernel(pos_ref, y_ref, x_ref, gate_ref, o_ref, buf_ref, sem):
    tc = x_ref.shape[0]
    base = pl.program_id(0) * tc

    def row_copy(r, k):
        return pltpu.make_async_copy(y_ref.at[pl.ds(pos_ref[2 * (base + r) + k], 1)],
                                     buf_ref.at[k, pl.ds(r, 1)], sem)

    def start(r, c):
        row_copy(r, 0).start(priority=0)
        row_copy(r, 1).start(priority=1)
        return c

    lax.fori_loop(0, tc, start, 0, unroll=8)
    pltpu.make_async_copy(buf_ref, buf_ref, sem).wait()
    gate = gate_ref[...]
    o_ref[...] = x_ref[...] + gate[:, 0:1] * buf_ref[0] + gate[:, 1:2] * buf_ref[1]


def _combine(y, pos, x, gate, tc):
    t, d = x.shape
    return pl.pallas_call(
        _combine_kernel,
        grid_spec=pltpu.PrefetchScalarGridSpec(
            num_scalar_prefetch=1,
            grid=(t // tc,),
            in_specs=[
                pl.BlockSpec(memory_space=pl.ANY),
                pl.BlockSpec((tc, d), lambda i, pos: (i, 0)),
                pl.BlockSpec((tc, LANES), lambda i, pos: (i, 0)),
            ],
            out_specs=pl.BlockSpec((tc, d), lambda i, pos: (i, 0)),
            scratch_shapes=[pltpu.VMEM((2, tc, d), F32), pltpu.SemaphoreType.DMA(())],
        ),
        out_shape=jax.ShapeDtypeStruct((t, d), F32),
        compiler_params=_params("arbitrary"),
        name="combine",
    )(pos, y, x, gate)


def _routing_plan(sel, tm):
    t = sel.shape[0]
    experts = sel[:, :2].reshape(-1)
    onehot = (experts[:, None] == jnp.arange(N_EXPERTS, dtype=jnp.int32)[None, :]).astype(jnp.int32)
    csum = jnp.cumsum(onehot, axis=0)
    rank = jnp.sum((csum - onehot) * onehot, axis=1)
    counts = csum[-1]
    padded = ((counts + tm - 1) // tm) * tm
    ends = jnp.cumsum(padded)
    starts = ends - padded
    pos = (starts[experts] + rank).astype(jnp.int32)
    n_tiles = (2 * t) // tm + N_EXPERTS + 1
    src =jnp.zeros((n_tiles * tm,), jnp.int32).at[pos].set(jnp.arange(2 * t, dtype=jnp.int32) // 2)
    tile_start = jnp.arange(n_tiles, dtype=jnp.int32) * tm
    tile_expert = jnp.sum((tile_start[:, None] >= ends[None, :]).astype(jnp.int32), axis=1)
    n_active = (ends[-1] // tm).astype(jnp.int32).reshape(1)
    last_expert = jnp.max(jnp.where(counts > 0, jnp.arange(N_EXPERTS, dtype=jnp.int32), 0))
    tile_expert = jnp.minimum(tile_expert, last_expert).astype(jnp.int32)
    return pos, src, tile_expert, n_active


def _rope_tables(seq):
    t = np.arange(seq)
    inv_freq = ROPE_THETA ** (-np.arange(ROPE_PAIRS, dtype=np.float64) / ROPE_PAIRS)
    row = (t // GRID_W)[:, None] * inv_freq
    col = (t % GRID_W)[:, None] * inv_freq
    cos = np.concatenate([np.cos(row), np.cos(row), np.cos(col), np.cos(col)], axis=1)
    sin = np.concatenate([-np.sin(row), np.sin(row), -np.sin(col), np.sin(col)], axis=1)
    return jnp.asarray(cos, F32), jnp.asarray(sin, F32)


def _tiles(seq):
    tm = min(512, seq)
    return dict(tm=tm, tq=min(512, seq), tk=min(512, seq), tf=512, te=min(512, seq), tc=min(256, seq))


def _trunk(x3, wts, experts):
    batch, seq, d = x3.shape
    t = batch * seq
    x = x3.reshape(t, d)
    cfg = _tiles(seq)
    cos, sin = _rope_tables(seq)
    depth = wts["ln1"].shape[0]
    cast_weights = None
    cast_tq = cfg["tq"] // 2
    if experts[0].dtype != BF16:
        raw = tuple(w[0] for w in experts)
        if experts[0].shape[0] == 1 and _cast_chunks(batch, seq, cast_tq, raw, N_EXPERTS) is not None:
            cast_weights = raw
        else:
            experts = tuple(w.astype(BF16) for w in experts)
    for l in range(depth):
        q, k, v1, u = _in_proj(x, wts["ln1"][l], wts["w_in"][l], wts["qg"][l], wts["kg"][l], wts["v_scale"][l],
                              cos, sin, seq, cfg["tm"])
        if l == 0 and cast_weights is not None:
            a, cast_out = _attention(wts["bounded"][l], q, k, v1, batch, seq, cast_tq, cfg["tk"], cast_weights)
            experts = tuple(w[None] for w in cast_out)
        else:
            a, _ = _attention(wts["bounded"][l], q, k, v1, batch, seq, cfg["tq"], cfg["tk"])
        f = _fourier(u, wts["w_fourier"][l], wts["fourier_out_norm"][l], batch, seq)
        i = l // 2
        if l % 2 == 0:
            x1, h2 = _out_proj(a, f, x, wts["attn_out_norm"][l], wts["w_out"][l], wts["ln2"][l], None, cfg["tm"])
            x = _ffn_dense(h2, wts["wg_dense"][i], wts["wu_dense"][i], wts["wd_dense"][i], x1, cfg["te"], cfg["tf"])
        else:
            x1, h2, sel, gate = _out_proj(a, f, x, wts["attn_out_norm"][l], wts["w_out"][l], wts["ln2"][l],
                                          wts["router"][i], cfg["tm"])
            pos, src, tile_expert, n_active = _routing_plan(sel, cfg["te"])
            ys = _ffn_experts(h2, src, experts[0][i], experts[1][i], experts[2][i], tile_expert, n_active,
                              cfg["te"], cfg["tf"])
            x = _combine(ys, pos, x1, gate, cfg["tc"])
    return x.reshape(batch, seq, d), experts


def _prepare_weights(ln1, w_in, q_norm, k_norm, w_fourier, attn_out_norm, fourier_out_norm, w_out, ln2,
                     w_gate_dense, w_up_dense, w_down_dense, router, w_gate_exp, w_up_exp, w_down_exp):
    depth = ln1.shape[0]
    row = lambda g: g.reshape(depth, 1, -1)
    router_padded = jnp.pad(router, ((0, 0), (0, 0), (0, LANES - router.shape[-1])))
    router_hi = router_padded.astype(BF16)
    router_lo = (router_padded - router_hi.astype(F32)).astype(BF16)
    score_bound = (LOG2_E * math.sqrt(HEAD_DIM) * (1 + 2.0 ** -6)
                   * jnp.max(jnp.abs(q_norm), axis=1) * jnp.max(jnp.abs(k_norm), axis=1))
    bounded = score_bound <= MAX_UNSHIFTED_SCORE
    v_scale = jnp.where(bounded, jnp.exp2(-jnp.ceil(score_bound)), 1.0)
    return dict(
        bounded=bounded.astype(jnp.int32).reshape(depth, 1),
        v_scale=jnp.broadcast_to(v_scale.reshape(depth, 1, 1), (depth, 1, HEAD_DIM)).astype(F32),
        ln1=row(ln1), w_in=w_in.astype(BF16),
        qg=row(q_norm * (LOG2_E * HEAD_DIM ** -0.5)),
        kg=row(k_norm),
        w_fourier=w_fourier.astype(BF16), attn_out_norm=row(attn_out_norm), fourier_out_norm=row(fourier_out_norm),
        w_out=w_out.astype(BF16), ln2=row(ln2),
        wg_dense=w_gate_dense.astype(BF16), wu_dense=w_up_dense.astype(BF16), wd_dense=w_down_dense.astype(BF16),
        router=jnp.concatenate([router_hi, router_lo], axis=2),
    )


def kernel(x_prompt, x_sample, ln1, w_in, q_norm, k_norm, w_fourier, attn_out_norm, fourier_out_norm, w_out, ln2,
           w_gate_dense, w_up_dense, w_down_dense, router, w_gate_exp, w_up_exp, w_down_exp):
    wts = _prepare_weights(ln1, w_in, q_norm, k_norm, w_fourier, attn_out_norm, fourier_out_norm, w_out, ln2,
                           w_gate_dense, w_up_dense, w_down_dense, router, w_gate_exp, w_up_exp, w_down_exp)
    y_prompt, experts = _trunk(x_prompt, wts, (w_gate_exp, w_up_exp, w_down_exp))
    y_sample, _ = _trunk(x_sample, wts, experts)
    return y_prompt, y_sample
```

```python
import functools
import math

import numpy as np
import jax
import jax.numpy as jnp
from jax import lax
from jax.experimental import pallas as pl
from jax.experimental.pallas import tpu as pltpu

F32 = jnp.float32
BF16 = jnp.bfloat16

HEAD_DIM = 128
N_Q_HEADS = 8
N_KV_HEADS = 2
Q_GROUP = N_Q_HEADS // N_KV_HEADS
ATTN_WIDTH = N_Q_HEADS * HEAD_DIM
KV_WIDTH = N_KV_HEADS * HEAD_DIM
GROUP_DIM = 128
N_FOURIER_GROUPS = 8
FOURIER_WIDTH = N_FOURIER_GROUPS * GROUP_DIM
N_EXPERTS = 8
GRID_W = 64
ROPE_THETA = 10000.0
ROPE_PAIRS = HEAD_DIM // 4
EPS = 1e-6
LOG2_E = 1.4426950408889634
MAX_UNSHIFTED_SCORE = 48.0
FAST_UNROLL = 8
FOURIER_BLOCK = 16
LANES = 128
VMEM_LIMIT = 56 * 1024 * 1024


def _params(*sem):
    return pltpu.CompilerParams(dimension_semantics=sem, vmem_limit_bytes=VMEM_LIMIT)


def _rms(x, g):
    return x * lax.rsqrt(jnp.mean(x * x, axis=-1, keepdims=True) + EPS) * g


def _const_spec(shape):
    zeros = (0,) * len(shape)
    return pl.BlockSpec(shape, lambda *_: zeros, pipeline_mode=pl.Buffered(1))


def _in_proj_kernel(x_ref, ln_ref, w_ref, qg_ref, kg_ref, vs_ref, cos_ref, sin_ref, q_ref, k_ref, v_ref, u_ref):
    h = _rms(x_ref[...], ln_ref[...]).astype(BF16)
    proj = jnp.dot(h, w_ref[...], preferred_element_type=F32)
    cos = cos_ref[...]
    sin = sin_ref[...]
    lane = lax.broadcasted_iota(jnp.int32, cos.shape, 1)
    lower = (lane & (2 * ROPE_PAIRS - 1)) < ROPE_PAIRS

    def norm_rope(t, g):
        y = _rms(t, g)
        partner = jnp.where(lower, pltpu.roll(y, HEAD_DIM - ROPE_PAIRS, 1), pltpu.roll(y, ROPE_PAIRS, 1))
        return y * cos + partner * sin

    qg = qg_ref[...]
    kg = kg_ref[...]
    for hd in range(N_Q_HEADS):
        sl = slice(hd * HEAD_DIM, (hd + 1) * HEAD_DIM)
        q_ref[:, sl] = norm_rope(proj[:, sl], qg).astype(q_ref.dtype)
    for hd in range(N_KV_HEADS):
        sl = slice(hd * HEAD_DIM, (hd + 1) * HEAD_DIM)
        k_ref[:, sl] = norm_rope(proj[:, ATTN_WIDTH + hd * HEAD_DIM:ATTN_WIDTH + (hd + 1) * HEAD_DIM], kg).astype(k_ref.dtype)
    vs = vs_ref[...]
    ones = jnp.broadcast_to(vs, (proj.shape[0], HEAD_DIM)).astype(v_ref.dtype)
    for hd in range(N_KV_HEADS):
        v0 = ATTN_WIDTH + KV_WIDTH + hd * HEAD_DIM
        v_ref[:, 2 * hd * HEAD_DIM:(2 * hd + 1) * HEAD_DIM] = (proj[:, v0:v0 + HEAD_DIM] * vs).astype(v_ref.dtype)
        v_ref[:, (2 * hd + 1) * HEAD_DIM:(2 * hd + 2) * HEAD_DIM] = ones
    u_ref[...] = proj[:, ATTN_WIDTH + 2 * KV_WIDTH:].astype(u_ref.dtype)


def _in_proj(x, ln, w, qg, kg, vs, cos, sin, seq, tm):
    t, d = x.shape
    in_width = w.shape[1]
    blocks_per_seq = seq // tm
    row = lambda i: (i, 0)
    pos = lambda i: (i % blocks_per_seq, 0)
    return pl.pallas_call(
        _in_proj_kernel,
        grid=(t // tm,),
        in_specs=[
            pl.BlockSpec((tm, d), row),
            _const_spec((1, d)),
            _const_spec((d, in_width)),
            _const_spec((1, HEAD_DIM)),
            _const_spec((1, HEAD_DIM)),
            _const_spec((1, HEAD_DIM)),
            pl.BlockSpec((tm, HEAD_DIM), pos),
            pl.BlockSpec((tm, HEAD_DIM), pos),
        ],
        out_specs=[
            pl.BlockSpec((tm, ATTN_WIDTH), row),
            pl.BlockSpec((tm, KV_WIDTH), row),
            pl.BlockSpec((tm, 2 * KV_WIDTH), row),
            pl.BlockSpec((tm, FOURIER_WIDTH), row),
        ],
        out_shape=[
            jax.ShapeDtypeStruct((t, ATTN_WIDTH), BF16),
            jax.ShapeDtypeStruct((t, KV_WIDTH), BF16),
            jax.ShapeDtypeStruct((t, 2 * KV_WIDTH), BF16),
            jax.ShapeDtypeStruct((t, FOURIER_WIDTH), BF16),
        ],
        compiler_params=_params("parallel"),
        name="in_proj",
    )(x, ln, w, qg, kg, vs, cos, sin)


def _attn_kernel(bounded_ref, q_ref, k_ref, v_ref, *rest, tk, n_cast):
    o_ref = rest[n_cast]
    for src, dst in zip(rest[:n_cast], rest[n_cast + 1:]):
        dst[...] = src[...].astype(dst.dtype)
    tq = q_ref.shape[0]
    seq = k_ref.shape[0]
    q = jnp.concatenate([q_ref[:, g * HEAD_DIM:(g + 1) * HEAD_DIM] for g in range(Q_GROUP)], axis=0)
    rows = q.shape[0]

    def scores(j):
        start = pl.multiple_of(j * tk, tk)
        kj = k_ref[pl.ds(start, tk), :]
        vj = v_ref[pl.ds(start, tk), :]
        return lax.dot_general(q, kj, (((1,), (1,)), ((), ())), preferred_element_type=F32), vj

    def write(out):
        for g in range(Q_GROUP):
            o_ref[:, g * HEAD_DIM:(g + 1) * HEAD_DIM] = out[g * tq:(g + 1) * tq].astype(o_ref.dtype)

    @pl.when(bounded_ref[0] == 1)
    def _():
        def body(j, acc):
            s, vj = scores(j)
            return acc + jnp.dot(jnp.exp2(s).astype(BF16), vj, preferred_element_type=F32)

        acc = lax.fori_loop(0, seq // tk, body, jnp.zeros((rows, 2 * HEAD_DIM), F32), unroll=FAST_UNROLL)
        write(acc[:, :HEAD_DIM] / acc[:, HEAD_DIM:])

    @pl.when(bounded_ref[0] == 0)
    def _():
        def body(j, carry):
            m, acc = carry
            s, vj = scores(j)
            m_new = jnp.maximum(m, jnp.max(s, axis=-1, keepdims=True))
            p = jnp.exp2(s - m_new).astype(BF16)
            return m_new, jnp.exp2(m - m_new) * acc + jnp.dot(p, vj, preferred_element_type=F32)

        m0 = jnp.full((rows, 1), -jnp.inf, F32)
        _, acc = lax.fori_loop(0, seq // tk, body, (m0, jnp.zeros((rows, 2 * HEAD_DIM), F32)))
        write(acc[:, :HEAD_DIM] / acc[:, HEAD_DIM:])


def _cast_chunks(batch, seq, tq, weights, n_experts):
    steps = batch * N_KV_HEADS * (seq // tq)
    if steps % n_experts:
        return None
    chunks = steps // n_experts
    if any(w.shape[1] % chunks or (w.shape[1] // chunks) % 16 for w in weights):
        return None
    return chunks


def _attention(bounded, q, k, v1, batch, seq, tq, tk, cast=None):
    t = q.shape[0]
    qb = seq // tq
    gw = Q_GROUP * HEAD_DIM
    in_specs = [
        pl.BlockSpec((tq, gw), lambda b, h, i, f: (b * qb + i, h)),
        pl.BlockSpec((seq, HEAD_DIM), lambda b, h, i, f: (b, h)),
        pl.BlockSpec((seq, 2 * HEAD_DIM), lambda b, h, i, f: (b, h)),
    ]
    out_specs = [pl.BlockSpec((tq, gw), lambda b, h, i, f: (b * qb + i, h))]
    out_shape = [jax.ShapeDtypeStruct((t, ATTN_WIDTH), BF16)]
    args = [bounded, q, k, v1]
    n_cast = 0
    if cast is not None:
        chunks = _cast_chunks(batch, seq, tq, cast, cast[0].shape[0])

        def chunk_index(b, h, i, f):
            step = (b * N_KV_HEADS + h) * qb + i
            return step // chunks, step % chunks, 0

        for w in cast:
            block = (1, w.shape[1] // chunks, w.shape[2])
            in_specs.append(pl.BlockSpec(block, chunk_index))
            out_specs.append(pl.BlockSpec(block, chunk_index))
            out_shape.append(jax.ShapeDtypeStruct(w.shape, BF16))
        args += list(cast)
        n_cast = len(cast)
    outs = pl.pallas_call(
        functools.partial(_attn_kernel, tk=tk, n_cast=n_cast),
        grid_spec=pltpu.PrefetchScalarGridSpec(
            num_scalar_prefetch=1,
            grid=(batch, N_KV_HEADS, qb),
            in_specs=in_specs,
            out_specs=out_specs,
        ),
        out_shape=out_shape,
        compiler_params=_params("parallel", "parallel", "parallel"),
        name="attention_cast" if n_cast else "attention",
    )(*args)
    return outs[0], tuple(outs[1:])


def _group(g):
    return slice(g * GROUP_DIM, (g + 1) * GROUP_DIM)


def _fourier1_kernel(u_ref, cmat_ref, dft_ref, twr_ref, twi_ref, are_ref, aim_ref, s_re, s_im):
    n1, fb, _ = u_ref.shape
    for g in range(N_FOURIER_GROUPS):
        s_re[g] = u_ref[:, :, _group(g)].astype(F32).reshape(n1 * fb, GROUP_DIM)
    lane = lax.broadcasted_iota(jnp.int32, twr_ref.shape, 1)
    for j in range(fb):
        rows = pl.ds(j, n1, stride=fb)
        zre, zim = [], []
        for g in range(N_FOURIER_GROUPS):
            r = jnp.dot(s_re[g, rows, :].astype(BF16), cmat_ref[...], preferred_element_type=F32)
            zre.append(r[:, :GROUP_DIM])
            zim.append(r[:, GROUP_DIM:])
        z = jnp.concatenate([jnp.concatenate(zre, axis=1), jnp.concatenate(zim, axis=1)], axis=0).astype(BF16)
        d = jnp.dot(dft_ref[...], z, preferred_element_type=F32)
        dre, dim = d[:n1], d[n1:]
        pick = lane == pl.program_id(1) * fb + j
        tr = jnp.sum(jnp.where(pick, twr_ref[...], 0.0), axis=1, keepdims=True)
        ti = jnp.sum(jnp.where(pick, twi_ref[...], 0.0), axis=1, keepdims=True)
        are = tr * dre - ti * dim
        aim = tr * dim + ti * dre
        for g in range(N_FOURIER_GROUPS):
            s_re[g, rows, :] = are[:, _group(g)]
            s_im[g, rows, :] = aim[:, _group(g)]
    for g in range(N_FOURIER_GROUPS):
        are_ref[:, :, _group(g)] = s_re[g].reshape(n1, fb, GROUP_DIM).astype(are_ref.dtype)
        aim_ref[:, :, _group(g)] = s_im[g].reshape(n1, fb, GROUP_DIM).astype(aim_ref.dtype)


def _fourier2_kernel(are_ref, aim_ref, dft_ref, wf_ref, g_ref, o_ref, s_out):
    fb, n2, width = are_ref.shape
    gain = g_ref[...]
    for k in range(fb):
        a = jnp.concatenate([are_ref[k], aim_ref[k]], axis=0)
        f = jnp.dot(dft_ref[...], a, preferred_element_type=F32).astype(BF16)
        ys = [jnp.dot(f[:, _group(g)], wf_ref[g], preferred_element_type=F32) for g in range(N_FOURIER_GROUPS)]
        ms = sum(jnp.sum(y * y, axis=-1, keepdims=True) for y in ys) * (1.0 / width)
        scale = lax.rsqrt(ms + EPS)
        for g in range(N_FOURIER_GROUPS):
            s_out[g, pl.ds(k, n2, stride=fb), :] = ys[g] * scale * gain[:, _group(g)]
    for g in range(N_FOURIER_GROUPS):
        o_ref[:, :, _group(g)] = s_out[g].reshape(n2, fb, GROUP_DIM).astype(o_ref.dtype)


def _fourier_tables(seq):
    n1 = 1 << (int(math.log2(seq)) // 2)
    n2 = seq // n1
    c = np.arange(GROUP_DIM)
    ang = 2 * np.pi * np.outer(c, c) / GROUP_DIM
    cmat = np.concatenate([np.cos(ang), -np.sin(ang)], axis=1) / math.sqrt(GROUP_DIM)
    a1 = 2 * np.pi * np.outer(np.arange(n1), np.arange(n1)) / n1
    d1 = np.block([[np.cos(a1), np.sin(a1)], [-np.sin(a1), np.cos(a1)]]) / math.sqrt(n1)
    psi = 2 * np.pi * np.outer(np.arange(n1), np.arange(n2)) / seq
    pad = (-n2) % LANES
    twr = np.pad(np.cos(psi), ((0, 0), (0, pad)))
    twi = np.pad(-np.sin(psi), ((0, 0), (0, pad)))
    a2 = 2 * np.pi * np.outer(np.arange(n2), np.arange(n2)) / n2
    d2 = np.concatenate([np.cos(a2), np.sin(a2)], axis=1) / math.sqrt(n2)
    return (n1, n2, jnp.asarray(cmat, BF16), jnp.asarray(d1, BF16), jnp.asarray(twr, F32),
            jnp.asarray(twi, F32), jnp.asarray(d2, BF16))


def _fourier(u, w_f, gain, batch, seq):
    t, width = u.shape
    n1, n2, cmat, d1, twr, twi, d2 = _fourier_tables(seq)
    fb = FOURIER_BLOCK
    assert n1 % fb == 0 and n2 % fb == 0
    u3 = u.reshape(batch * n1, n2, width)
    are, aim = pl.pallas_call(
        _fourier1_kernel,
        grid=(batch, n2 // fb),
        in_specs=[
            pl.BlockSpec((n1, fb, width), lambda b, j: (b, j, 0)),
            _const_spec(cmat.shape),
            _const_spec(d1.shape),
            _const_spec(twr.shape),
            _const_spec(twi.shape),
        ],
        out_specs=[pl.BlockSpec((n1, fb, width), lambda b, j: (b, j, 0))] * 2,
        out_shape=[jax.ShapeDtypeStruct((batch * n1, n2, width), BF16)] * 2,
        scratch_shapes=[pltpu.VMEM((N_FOURIER_GROUPS, n1 * fb, GROUP_DIM), F32)] * 2,
        compiler_params=_params("parallel", "parallel"),
        name="fourier_stage1",
    )(u3, cmat, d1, twr, twi)
    out = pl.pallas_call(
        _fourier2_kernel,
        grid=(batch, n1 // fb),
        in_specs=[
            pl.BlockSpec((fb, n2, width), lambda b, j: (b * (n1 // fb) + j, 0, 0)),
            pl.BlockSpec((fb, n2, width), lambda b, j: (b * (n1 // fb) + j, 0, 0)),
            _const_spec(d2.shape),
            _const_spec(w_f.shape),
            _const_spec(gain.shape),
        ],
        out_specs=pl.BlockSpec((n2, fb, width), lambda b, j: (b, j, 0)),
        out_shape=jax.ShapeDtypeStruct((batch * n2, n1, width), BF16),
        scratch_shapes=[pltpu.VMEM((N_FOURIER_GROUPS, n2 * fb, GROUP_DIM), F32)],
        compiler_params=_params("parallel", "parallel"),
        name="fourier_stage2",
    )(are, aim, d2, w_f, gain)
    return out.reshape(t, width)


def _out_proj_kernel(a_ref, f_ref, x_ref, ag_ref, w_ref, ln2_ref, *rest, routed):
    if routed:
        router_ref, x1_ref, h2_ref, sel_ref, gate_ref = rest
    else:
        x1_ref, h2_ref = rest
    tm = x_ref.shape[0]
    for part in range(2):
        rows = slice(part * (tm // 2), (part + 1) * (tm // 2))
        an = _rms(a_ref[rows, :].astype(F32), ag_ref[...]).astype(BF16)
        y = jnp.dot(an, w_ref[:ATTN_WIDTH, :], preferred_element_type=F32)
        y = y + jnp.dot(f_ref[rows, :], w_ref[ATTN_WIDTH:, :], preferred_element_type=F32)
        x1 = x_ref[rows, :] + y
        x1_ref[rows, :] = x1
        h2 = _rms(x1, ln2_ref[...])
        h2_ref[rows, :] = h2.astype(h2_ref.dtype)
        if routed:
            h_hi = h2.astype(BF16)
            h_lo = (h2 - h_hi.astype(F32)).astype(BF16)
            both = jnp.dot(h_hi, router_ref[...], preferred_element_type=F32)
            logits = (both[:, :LANES] + both[:, LANES:]
                      + jnp.dot(h_lo, router_ref[:, :LANES], preferred_element_type=F32))
            lane = lax.broadcasted_iota(jnp.int32, logits.shape, 1)
            lg = jnp.where(lane < N_EXPERTS, logits, -jnp.inf)
            m1 = jnp.max(lg, axis=-1, keepdims=True)
            i1 = jnp.min(jnp.where(lg == m1, lane, LANES), axis=-1, keepdims=True)
            lg2 = jnp.where(lane == i1, -jnp.inf, lg)
            m2 = jnp.max(lg2, axis=-1, keepdims=True)
            i2 = jnp.min(jnp.where(lg2 == m2, lane, LANES), axis=-1, keepdims=True)
            e2 = jnp.exp(m2 - m1)
            g1 = 1.0 / (1.0 + e2)
            g2 = e2 * g1
            sel_ref[rows, :] = jnp.where(lane == 0, i1, jnp.where(lane == 1, i2, 0))
            gate_ref[rows, :] = jnp.where(lane == 0, g1, jnp.where(lane == 1, g2, 0.0))


def _out_proj(a, f, x, ag, w, ln2, router, tm):
    t, d = x.shape
    routed = router is not None
    row = lambda i: (i, 0)
    in_specs = [
        pl.BlockSpec((tm, ATTN_WIDTH), row),
        pl.BlockSpec((tm, FOURIER_WIDTH), row),
        pl.BlockSpec((tm, d), row),
        _const_spec((1, ATTN_WIDTH)),
        _const_spec(w.shape),
        _const_spec((1, d)),
    ]
    out_specs = [pl.BlockSpec((tm, d), row), pl.BlockSpec((tm, d), row)]
    out_shape = [jax.ShapeDtypeStruct((t, d), F32), jax.ShapeDtypeStruct((t, d), F32 if routed else BF16)]
    args = [a, f, x, ag, w, ln2]
    if routed:
        in_specs.append(_const_spec(router.shape))
        out_specs += [pl.BlockSpec((tm, LANES), row)] * 2
        out_shape += [jax.ShapeDtypeStruct((t, LANES), jnp.int32), jax.ShapeDtypeStruct((t, LANES), F32)]
        args.append(router)
    return pl.pallas_call(
        functools.partial(_out_proj_kernel, routed=routed),
        grid=(t // tm,),
        in_specs=in_specs,
        out_specs=out_specs,
        out_shape=out_shape,
        compiler_params=_params("parallel"),
        name="out_proj_routed" if routed else "out_proj",
    )(*args)


def _swiglu_accumulate(h_ref, wg_ref, wu_ref, wd_ref, acc_ref):
    h = h_ref[...]
    g = jnp.dot(h, wg_ref[...], preferred_element_type=F32)
    u = jnp.dot(h, wu_ref[...], preferred_element_type=F32)
    act = (g * jax.nn.sigmoid(g) * u).astype(BF16)
    acc_ref[...] += jnp.dot(act, wd_ref[...], preferred_element_type=F32)


def _ffn_dense_kernel(h_ref, wg_ref, wu_ref, wd_ref, x_ref, o_ref):
    @pl.when(pl.program_id(1) == 0)
    def _():
        o_ref[...] = x_ref[...]

    _swiglu_accumulate(h_ref, wg_ref, wu_ref, wd_ref, o_ref)


def _ffn_dense(h, wg, wu, wd, x, tm, tf):
    r, d = h.shape
    d_ff = wg.shape[1]
    return pl.pallas_call(
        _ffn_dense_kernel,
        grid=(r // tm, d_ff // tf),
        in_specs=[
            pl.BlockSpec((tm, d), lambda i, j: (i, 0)),
            pl.BlockSpec((d, tf), lambda i, j: (0, j)),
            pl.BlockSpec((d, tf), lambda i, j: (0, j)),
            pl.BlockSpec((tf, d), lambda i, j: (j, 0)),
            pl.BlockSpec((tm, d), lambda i, j: (i, 0)),
        ],
        out_specs=pl.BlockSpec((tm, d), lambda i, j: (i, 0)),
        out_shape=jax.ShapeDtypeStruct((r, d), F32),
        compiler_params=_params("parallel", "arbitrary"),
        name="ffn_dense",
    )(h, wg, wu, wd, x)


def _ffn_experts_kernel(te_ref, na_ref, src_ref, h_ref, wg_ref, wu_ref, wd_ref, o_ref,
                        rows_ref, hb_ref, sem, *, nj):
    i = pl.program_id(0)
    j = pl.program_id(1)
    tm = hb_ref.shape[0]
    n_active = na_ref[0]
    slot = lax.rem(i, 2)
    per_step = -(-tm // nj)
    every_step = tm - per_step * (nj - 1)

    def row_copy(tile, r, s):
        return pltpu.make_async_copy(h_ref.at[pl.ds(src_ref[tile * tm + r], 1)], rows_ref.at[s, pl.ds(r, 1)],
                                     sem.at[s])

    @pl.when(j == 0)
    def _():
        @pl.when(i == 0)
        def _():
            def start(r, c):
                row_copy(0, r, 0).start()
                return c
            lax.fori_loop(0, tm, start, 0)

        @pl.when(i <= n_active)
        def _():
            pltpu.make_async_copy(rows_ref.at[slot], rows_ref.at[slot], sem.at[slot]).wait()

    @pl.when(i < n_active)
    def _():
        @pl.when(j == 0)
        def _():
            hb_ref[...] = rows_ref[slot].astype(BF16)
            o_ref[...] = jnp.zeros_like(o_ref)

        base = j * per_step

        @pl.when(j < nj - 1)
        def _():
            for r in range(every_step, per_step):
                row_copy(i + 1, base + r, 1 - slot).start()

        for r in range(every_step):
            row_copy(i + 1, base + r, 1 - slot).start()
        _swiglu_accumulate(hb_ref, wg_ref.at[0], wu_ref.at[0], wd_ref.at[0], o_ref)

    @pl.when(jnp.logical_and(i >= n_active, j == nj - 1))
    def _():
        o_ref[...] = jnp.zeros_like(o_ref)


def _ffn_experts(h, src, wg, wu, wd, tile_expert, n_active, tm, tf):
    d = h.shape[1]
    d_ff = wg.shape[2]
    nj = d_ff // tf
    n_tiles = src.shape[0] // tm

    def jj(i, j, na):
        return jnp.where(i < na[0], j, nj - 1)

    return pl.pallas_call(
        functools.partial(_ffn_experts_kernel, nj=nj),
        grid_spec=pltpu.PrefetchScalarGridSpec(
            num_scalar_prefetch=3,
            grid=(n_tiles, nj),
            in_specs=[
                pl.BlockSpec(memory_space=pl.ANY),
                pl.BlockSpec((1, d, tf), lambda i, j, te, na, src: (te[i], 0, jj(i, j, na))),
                pl.BlockSpec((1, d, tf), lambda i, j, te, na, src: (te[i], 0, jj(i, j, na))),
                pl.BlockSpec((1, tf, d), lambda i, j, te, na, src: (te[i], jj(i, j, na), 0)),
            ],
            out_specs=pl.BlockSpec((tm, d), lambda i, j, te, na, src: (i, 0)),
            scratch_shapes=[pltpu.VMEM((2, tm, d), F32), pltpu.VMEM((tm, d), BF16), pltpu.SemaphoreType.DMA((2,))],
        ),
        out_shape=jax.ShapeDtypeStruct((n_tiles * tm, d), F32),
        compiler_params=_params("arbitrary", "arbitrary"),
        name="ffn_experts",
    )(tile_expert, n_active, src, h, wg, wu, wd)


def _combine_kernel(pos_ref, y_ref, x_ref, gate_ref, o_ref, buf_ref, sem):
    tc = x_ref.shape[0]
    base = pl.program_id(0) * tc

    def row_copy(r, k):
        return pltpu.make_async_copy(y_ref.at[pl.ds(pos_ref[2 * (base + r) + k], 1)],
                                     buf_ref.at[k, pl.ds(r, 1)], sem)

    def start(r, c):
        row_copy(r, 0).start(priority=0)
        row_copy(r, 1).start(priority=1)
        return c

    lax.fori_loop(0, tc, start, 0, unroll=8)
    pltpu.make_async_copy(buf_ref, buf_ref, sem).wait()
    gate = gate_ref[...]
    o_ref[...] = x_ref[...] + gate[:, 0:1] * buf_ref[0] + gate[:, 1:2] * buf_ref[1]


def _combine(y, pos, x, gate, tc):
    t, d = x.shape
    return pl.pallas_call(
        _combine_kernel,
        grid_spec=pltpu.PrefetchScalarGridSpec(
            num_scalar_prefetch=1,
            grid=(t // tc,),
            in_specs=[
                pl.BlockSpec(memory_space=pl.ANY),
                pl.BlockSpec((tc, d), lambda i, pos: (i, 0)),
                pl.BlockSpec((tc, LANES), lambda i, pos: (i, 0)),
            ],
            out_specs=pl.BlockSpec((tc, d), lambda i, pos: (i, 0)),
            scratch_shapes=[pltpu.VMEM((2, tc, d), F32), pltpu.SemaphoreType.DMA(())],
        ),
        out_shape=jax.ShapeDtypeStruct((t, d), F32),
        compiler_params=_params("arbitrary"),
        name="combine",
    )(pos, y, x, gate)


def _routing_plan(sel, tm):
    t = sel.shape[0]
    experts = sel[:, :2].reshape(-1)
    onehot = (experts[:, None] == jnp.arange(N_EXPERTS, dtype=jnp.int32)[None, :]).astype(jnp.int32)
    csum = jnp.cumsum(onehot, axis=0)
    rank = jnp.sum((csum - onehot) * onehot, axis=1)
    counts = csum[-1]
    padded = ((counts + tm - 1) // tm) * tm
    ends = jnp.cumsum(padded)
    starts = ends - padded
    pos = (starts[experts] + rank).astype(jnp.int32)
    n_tiles = (2 * t) // tm + N_EXPERTS + 1
    src =jnp.zeros((n_tiles * tm,), jnp.int32).at[pos].set(jnp.arange(2 * t, dtype=jnp.int32) // 2)
    tile_start = jnp.arange(n_tiles, dtype=jnp.int32) * tm
    tile_expert = jnp.sum((tile_start[:, None] >= ends[None, :]).astype(jnp.int32), axis=1)
    n_active = (ends[-1] // tm).astype(jnp.int32).reshape(1)
    last_expert = jnp.max(jnp.where(counts > 0, jnp.arange(N_EXPERTS, dtype=jnp.int32), 0))
    tile_expert = jnp.minimum(tile_expert, last_expert).astype(jnp.int32)
    return pos, src, tile_expert, n_active


def _rope_tables(seq):
    t = np.arange(seq)
    inv_freq = ROPE_THETA ** (-np.arange(ROPE_PAIRS, dtype=np.float64) / ROPE_PAIRS)
    row = (t // GRID_W)[:, None] * inv_freq
    col = (t % GRID_W)[:, None] * inv_freq
    cos = np.concatenate([np.cos(row), np.cos(row), np.cos(col), np.cos(col)], axis=1)
    sin = np.concatenate([-np.sin(row), np.sin(row), -np.sin(col), np.sin(col)], axis=1)
    return jnp.asarray(cos, F32), jnp.asarray(sin, F32)


def _tiles(seq):
    tm = min(512, seq)
    return dict(tm=tm, tq=min(512, seq), tk=min(512, seq), tf=512, te=min(512, seq), tc=min(256, seq))


def _trunk(x3, wts, experts):
    batch, seq, d = x3.shape
    t = batch * seq
    x = x3.reshape(t, d)
    cfg = _tiles(seq)
    cos, sin = _rope_tables(seq)
    depth = wts["ln1"].shape[0]
    cast_weights = None
    cast_tq = cfg["tq"] // 2
    if experts[0].dtype != BF16:
        raw = tuple(w[0] for w in experts)
        if experts[0].shape[0] == 1 and _cast_chunks(batch, seq, cast_tq, raw, N_EXPERTS) is not None:
            cast_weights = raw
        else:
            experts = tuple(w.astype(BF16) for w in experts)
    for l in range(depth):
        q, k, v1, u = _in_proj(x, wts["ln1"][l], wts["w_in"][l], wts["qg"][l], wts["kg"][l], wts["v_scale"][l],
                              cos, sin, seq, cfg["tm"])
        if l == 0 and cast_weights is not None:
            a, cast_out = _attention(wts["bounded"][l], q, k, v1, batch, seq, cast_tq, cfg["tk"], cast_weights)
            experts = tuple(w[None] for w in cast_out)
        else:
            a, _ = _attention(wts["bounded"][l], q, k, v1, batch, seq, cfg["tq"], cfg["tk"])
        f = _fourier(u, wts["w_fourier"][l], wts["fourier_out_norm"][l], batch, seq)
        i = l // 2
        if l % 2 == 0:
            x1, h2 = _out_proj(a, f, x, wts["attn_out_norm"][l], wts["w_out"][l], wts["ln2"][l], None, cfg["tm"])
            x = _ffn_dense(h2, wts["wg_dense"][i], wts["wu_dense"][i], wts["wd_dense"][i], x1, cfg["te"], cfg["tf"])
        else:
            x1, h2, sel, gate = _out_proj(a, f, x, wts["attn_out_norm"][l], wts["w_out"][l], wts["ln2"][l],
                                          wts["router"][i], cfg["tm"])
            pos, src, tile_expert, n_active = _routing_plan(sel, cfg["te"])
            ys = _ffn_experts(h2, src, experts[0][i], experts[1][i], experts[2][i], tile_expert, n_active,
                              cfg["te"], cfg["tf"])
            x = _combine(ys, pos, x1, gate, cfg["tc"])
    return x.reshape(batch, seq, d), experts


def _prepare_weights(ln1, w_in, q_norm, k_norm, w_fourier, attn_out_norm, fourier_out_norm, w_out, ln2,
                     w_gate_dense, w_up_dense, w_down_dense, router, w_gate_exp, w_up_exp, w_down_exp):
    depth = ln1.shape[0]
    row = lambda g: g.reshape(depth, 1, -1)
    router_padded = jnp.pad(router, ((0, 0), (0, 0), (0, LANES - router.shape[-1])))
    router_hi = router_padded.astype(BF16)
    router_lo = (router_padded - router_hi.astype(F32)).astype(BF16)
    score_bound = (LOG2_E * math.sqrt(HEAD_DIM) * (1 + 2.0 ** -6)
                   * jnp.max(jnp.abs(q_norm), axis=1) * jnp.max(jnp.abs(k_norm), axis=1))
    bounded = score_bound <= MAX_UNSHIFTED_SCORE
    v_scale = jnp.where(bounded, jnp.exp2(-jnp.ceil(score_bound)), 1.0)
    return dict(
        bounded=bounded.astype(jnp.int32).reshape(depth, 1),
        v_scale=jnp.broadcast_to(v_scale.reshape(depth, 1, 1), (depth, 1, HEAD_DIM)).astype(F32),
        ln1=row(ln1), w_in=w_in.astype(BF16),
        qg=row(q_norm * (LOG2_E * HEAD_DIM ** -0.5)),
        kg=row(k_norm),
        w_fourier=w_fourier.astype(BF16), attn_out_norm=row(attn_out_norm), fourier_out_norm=row(fourier_out_norm),
        w_out=w_out.astype(BF16), ln2=row(ln2),
        wg_dense=w_gate_dense.astype(BF16), wu_dense=w_up_dense.astype(BF16), wd_dense=w_down_dense.astype(BF16),
        router=jnp.concatenate([router_hi, router_lo], axis=2),
    )


def kernel(x_prompt, x_sample, ln1, w_in, q_norm, k_norm, w_fourier, attn_out_norm, fourier_out_norm, w_out, ln2,
           w_gate_dense, w_up_dense, w_down_dense, router, w_gate_exp, w_up_exp, w_down_exp):
    wts = _prepare_weights(ln1, w_in, q_norm, k_norm, w_fourier, attn_out_norm, fourier_out_norm, w_out, ln2,
                           w_gate_dense, w_up_dense, w_down_dense, router, w_gate_exp, w_up_exp, w_down_exp)
    y_prompt, experts = _trunk(x_prompt, wts, (w_gate_exp, w_up_exp, w_down_exp))
    y_sample, _ = _trunk(x_sample, wts, experts)
    return y_prompt, y_sample
```
